```python
import jax, jax.numpy as jnp
from jax import lax
import numpy as np

D_MODEL = 1024
BATCH = 2
SEQ = 8192
DEPTH = 4

N_MIXERS = 3
NORM_EPS = 1e-6
L2_EPS = 1e-6
BRANCH = 2 * D_MODEL

GDN_DK = 128
GDN_HEADS = D_MODEL // GDN_DK
GDN_DV = BRANCH // GDN_HEADS
GDN_QK = GDN_HEADS * GDN_DK
GDN_V = GDN_HEADS * GDN_DV
GDN_CONV = 4
GDN_CHUNK = 64
GDN_SPLITS = (GDN_QK, GDN_QK, GDN_V, GDN_V, GDN_HEADS, GDN_HEADS)
GDN_IN = sum(GDN_SPLITS)

HGRN_DK = 128
HGRN_HEADS = D_MODEL // HGRN_DK
HGRN_DV = BRANCH // HGRN_HEADS
HGRN_F = HGRN_HEADS * HGRN_DK
HGRN_V = HGRN_HEADS * HGRN_DV
HGRN_CHUNK = 16
HGRN_SPLITS = (HGRN_F, HGRN_F, HGRN_V, HGRN_V)
HGRN_IN = sum(HGRN_SPLITS)

RWKV_DH = 64
RWKV_HEADS = BRANCH // RWKV_DH
RWKV_DECAY_LORA = 64
RWKV_AAA_LORA = 64
RWKV_SPLITS = (BRANCH, BRANCH, BRANCH, BRANCH, RWKV_DECAY_LORA, RWKV_AAA_LORA)
RWKV_IN = sum(RWKV_SPLITS)
RWKV_GN_EPS = RWKV_DH * 1e-5

N_GDN = (DEPTH + N_MIXERS - 1) // N_MIXERS
N_HGRN = (DEPTH + N_MIXERS - 2) // N_MIXERS
N_RWKV = DEPTH // N_MIXERS

kernel_name = "hybrid_gdn_hgrn2_rwkv7_interleaved"


def split_cols(p, sizes):
    idx = [int(s) for s in np.cumsum(sizes)[:-1]]
    return jnp.split(p, idx, axis=-1)


def rms_norm(x, w, eps=NORM_EPS):
    xf = x.astype(jnp.float32)
    return xf * lax.rsqrt(jnp.mean(xf * xf, axis=-1, keepdims=True) + eps) * w


def l2_norm(x, eps=L2_EPS):
    return x * lax.rsqrt(jnp.sum(x * x, axis=-1, keepdims=True) + eps)


def causal_short_conv(x, w):
    K = w.shape[0]
    T = x.shape[1]
    xp = jnp.pad(x, ((0, 0), (K - 1, 0), (0, 0)))
    out = xp[:, 0:T] * w[0]
    for j in range(1, K):
        out = out + xp[:, j:j + T] * w[j]
    return out


def to_chunks(t, size):
    b, T, h = t.shape[:3]
    t = t.reshape((b, T // size, size, h) + t.shape[3:])
    return jnp.moveaxis(t, 3, 1)


def from_chunks(t):
    b, h, n, c, d = t.shape
    return t.transpose(0, 2, 3, 1, 4).reshape(b, n * c, h, d)


def gated_delta_rule_chunked(q, k, v, beta, g):
    bsz, T, H, dk = q.shape
    dv = v.shape[-1]
    C = GDN_CHUNK
    q, k, v = to_chunks(q, C), to_chunks(k, C), to_chunks(v, C)
    beta, g = to_chunks(beta, C), to_chunks(g, C)
    gc = jnp.cumsum(g, axis=-1)
    incl = jnp.tril(jnp.ones((C, C), bool))
    strict = jnp.tril(jnp.ones((C, C), bool), -1)
    rel = gc[..., :, None] - gc[..., None, :]
    decay = jnp.exp(jnp.where(incl, rel, -jnp.inf))
    kb = k * beta[..., None]
    A = jnp.where(strict, jnp.einsum('bhnid,bhnjd->bhnij', kb, k) * decay, 0.0)
    tri = A + jnp.eye(C, dtype=A.dtype)
    rhs = jnp.concatenate([v * beta[..., None], kb * jnp.exp(gc)[..., None]], axis=-1)
    sol = lax.linalg.triangular_solve(tri, rhs, left_side=True, lower=True, unit_diagonal=True)
    u, w = sol[..., :dv], sol[..., dv:]
    attn = jnp.einsum('bhnid,bhnjd->bhnij', q, k) * decay
    q_dec = q * jnp.exp(gc)[..., None]
    k_dec = k * jnp.exp(gc[..., -1:] - gc)[..., None]
    g_last = jnp.exp(gc[..., -1])

    def step(S, inp):
        q_c, a_c, u_c, w_c, k_c, gl_c = inp
        v_new = u_c - jnp.einsum('bhck,bhkv->bhcv', w_c, S)
        o_c = jnp.einsum('bhck,bhkv->bhcv', q_c, S) + jnp.einsum('bhij,bhjv->bhiv', a_c, v_new)
        S = S * gl_c[..., None, None] + jnp.einsum('bhck,bhcv->bhkv', k_c, v_new)
        return S, o_c

    xs = tuple(jnp.moveaxis(t, 2, 0) for t in (q_dec, attn, u, w, k_dec, g_last))
    S0 = jnp.zeros((bsz, H, dk, dv), jnp.float32)
    _, o = lax.scan(step, S0, xs)
    return from_chunks(jnp.moveaxis(o, 0, 2))


def gdn_mixer(h, w_in, conv_w, a_log, dt_bias, norm_w, w_out):
    bsz, T, _ = h.shape
    p = (h @ w_in).astype(jnp.float32)
    qkv, z, b, a = split_cols(p, (2 * GDN_QK + GDN_V, GDN_V, GDN_HEADS, GDN_HEADS))
    qkv = jax.nn.silu(causal_short_conv(qkv, conv_w))
    q, k, v = split_cols(qkv, (GDN_QK, GDN_QK, GDN_V))
    q = l2_norm(q.reshape(bsz, T, GDN_HEADS, GDN_DK)) * (GDN_DK ** -0.5)
    k = l2_norm(k.reshape(bsz, T, GDN_HEADS, GDN_DK))
    v = v.reshape(bsz, T, GDN_HEADS, GDN_DV)
    beta = jax.nn.sigmoid(b)
    g = -jnp.exp(a_log) * jax.nn.softplus(a + dt_bias)
    o = gated_delta_rule_chunked(q, k, v, beta, g)
    o = rms_norm(o, norm_w) * jax.nn.silu(z.reshape(bsz, T, GDN_HEADS, GDN_DV))
    return o.reshape(bsz, T, GDN_V) @ w_out


def gla_chunked(q, k, v, log_f):
    bsz, T, H, dk = q.shape
    dv = v.shape[-1]
    C = HGRN_CHUNK
    q, k, v, lf = (to_chunks(t, C) for t in (q, k, v, log_f))
    G = jnp.cumsum(lf, axis=3)
    ref = G[..., C // 2:C // 2 + 1, :]
    incl = jnp.tril(jnp.ones((C, C), bool))
    attn = jnp.einsum('bhnid,bhnjd->bhnij', q * jnp.exp(G - ref), k * jnp.exp(ref - G))
    attn = jnp.where(incl, attn, 0.0)
    o_intra = jnp.einsum('bhnij,bhnjv->bhniv', attn, v)
    q_dec = q * jnp.exp(G)
    k_dec = k * jnp.exp(G[..., -1:, :] - G)
    g_last = jnp.exp(G[..., -1, :])

    def step(S, inp):
        q_c, k_c, v_c, gl_c = inp
        o_c = jnp.einsum('bhck,bhkv->bhcv', q_c, S)
        S = S * gl_c[..., None] + jnp.einsum('bhck,bhcv->bhkv', k_c, v_c)
        return S, o_c

    xs = tuple(jnp.moveaxis(t, 2, 0) for t in (q_dec, k_dec, v, g_last))
    S0 = jnp.zeros((bsz, H, dk, dv), jnp.float32)
    _, o_inter = lax.scan(step, S0, xs)
    return from_chunks(o_intra + jnp.moveaxis(o_inter, 0, 2))


def hgrn2_mixer(h, w_in, lower_bound, norm_w, w_out):
    bsz, T, _ = h.shape
    p = (h @ w_in).astype(jnp.float32)
    q, f, i, z = split_cols(p, HGRN_SPLITS)
    lb = lower_bound
    log_f = jnp.logaddexp(jnp.log(lb), jnp.log1p(-lb) + jax.nn.log_sigmoid(f))
    k = (1.0 - lb) * jax.nn.sigmoid(-f)
    q = jax.nn.silu(q)
    shp_k = (bsz, T, HGRN_HEADS, HGRN_DK)
    o = gla_chunked(q.reshape(shp_k), k.reshape(shp_k),
                    i.reshape(bsz, T, HGRN_HEADS, HGRN_DV), log_f.reshape(shp_k))
    o = rms_norm(o, norm_w) * jax.nn.silu(z.reshape(bsz, T, HGRN_HEADS, HGRN_DV))
    return o.reshape(bsz, T, HGRN_V) @ w_out


def rwkv7_scan(r, w, k, v, kk, a):
    bsz, T, H, N = r.shape

    def step(S, inp):
        r_t, w_t, k_t, v_t, kk_t, a_t = inp
        sa = jnp.einsum('bhvk,bhk->bhv', S, -kk_t)
        S = (S * w_t[:, :, None, :] + sa[..., None] * (kk_t * a_t)[:, :, None, :]
             + v_t[..., None] * k_t[:, :, None, :])
        return S, jnp.einsum('bhvk,bhk->bhv', S, r_t)

    xs = tuple(jnp.moveaxis(t, 1, 0) for t in (r, w, k, v, kk, a))
    S0 = jnp.zeros((bsz, H, N, N), jnp.float32)
    _, y = lax.scan(step, S0, xs)
    return jnp.moveaxis(y, 0, 1)


def rwkv7_mixer(h, w_in, mu, w0, w_lora_up, a0, a_lora_up, k_k, k_a, r_k, ln_w, ln_b, w_out):
    bsz, T, _ = h.shape
    p = (h @ w_in).astype(jnp.float32)
    p_prev = jnp.pad(p, ((0, 0), (1, 0), (0, 0)))[:, :-1]
    p = p + (p_prev - p) * mu
    r, k, v, z, wd, ad = split_cols(p, RWKV_SPLITS)
    w_log = -jax.nn.softplus(-(w0 + jnp.tanh(wd) @ w_lora_up)) - 0.5
    decay = jnp.exp(-jnp.exp(w_log))
    a = jax.nn.sigmoid(a0 + ad @ a_lora_up)
    shp = (bsz, T, RWKV_HEADS, RWKV_DH)
    kk = l2_norm((k * k_k).reshape(shp), eps=1e-12)
    k = k * (1.0 + (a - 1.0) * k_a)
    r, k, v, decay, a = (t.reshape(shp) for t in (r, k, v, decay, a))
    y = rwkv7_scan(r, decay, k, v, kk, a)
    mean = jnp.mean(y, axis=-1, keepdims=True)
    var = jnp.mean(jnp.square(y - mean), axis=-1, keepdims=True)
    y = ((y - mean) * lax.rsqrt(var + RWKV_GN_EPS)).reshape(bsz, T, BRANCH) * ln_w + ln_b
    bonus = jnp.sum(r * k * r_k, axis=-1, keepdims=True) * v
    y = (y + bonus.reshape(bsz, T, BRANCH)) * jax.nn.silu(z)
    return y @ w_out


def setup_inputs(seed: int = 0) -> dict:
    key = jax.random.key(seed)
    ks = iter(jax.random.split(key, 32))
    nrm = lambda shape, s: jax.random.normal(next(ks), shape, jnp.float32) * s
    uni = lambda shape, lo, hi: jax.random.uniform(next(ks), shape, jnp.float32, lo, hi)
    D = D_MODEL
    dt = jnp.exp(uni((N_GDN, GDN_HEADS), float(np.log(1e-3)), float(np.log(1e-1))))
    return {
        "x": nrm((BATCH, SEQ, D), 1.0),
        "c": nrm((BATCH, D), 1.0),
        "ada_w": nrm((DEPTH, D, 3 * D), 0.5 * D ** -0.5),
        "ada_b": nrm((DEPTH, 3 * D), 0.1),
        "norm_w": 1.0 + nrm((DEPTH, D), 0.02),
        "gdn_w_in": nrm((N_GDN, D, GDN_IN), D ** -0.5),
        "gdn_conv_w": nrm((N_GDN, GDN_CONV, 2 * GDN_QK + GDN_V), GDN_CONV ** -0.5),
        "gdn_a_log": jnp.log(uni((N_GDN, GDN_HEADS), 1.0, 16.0)),
        "gdn_dt_bias": dt + jnp.log(-jnp.expm1(-dt)),
        "gdn_norm_w": 1.0 + nrm((N_GDN, GDN_DV), 0.02),
        "gdn_w_out": nrm((N_GDN, GDN_V, D), GDN_V ** -0.5),
        "hgrn_w_in": nrm((N_HGRN, D, HGRN_IN), D ** -0.5),
        "hgrn_lb_logits": nrm((DEPTH, HGRN_F), 0.1),
        "hgrn_norm_w": 1.0 + nrm((N_HGRN, HGRN_DV), 0.02),
        "hgrn_w_out": nrm((N_HGRN, HGRN_V, D), HGRN_V ** -0.5),
        "rwkv_w_in": nrm((N_RWKV, D, RWKV_IN), D ** -0.5),
        "rwkv_mu": uni((N_RWKV, RWKV_IN), 0.0, 1.0),
        "rwkv_w0": uni((N_RWKV, BRANCH), -6.0, -1.0),
        "rwkv_w_lora_up": nrm((N_RWKV, RWKV_DECAY_LORA, BRANCH), 0.5 * RWKV_DECAY_LORA ** -0.5),
        "rwkv_a0": nrm((N_RWKV, BRANCH), 0.1),
        "rwkv_a_lora_up": nrm((N_RWKV, RWKV_AAA_LORA, BRANCH), 0.5 * RWKV_AAA_LORA ** -0.5),
        "rwkv_k_k": 0.85 + nrm((N_RWKV, BRANCH), 0.02),
        "rwkv_k_a": 1.0 + nrm((N_RWKV, BRANCH), 0.02),
        "rwkv_r_k": nrm((N_RWKV, RWKV_HEADS, RWKV_DH), 0.1),
        "rwkv_ln_w": 1.0 + nrm((N_RWKV, BRANCH), 0.02),
        "rwkv_ln_b": nrm((N_RWKV, BRANCH), 0.02),
        "rwkv_w_out": nrm((N_RWKV, BRANCH, D), BRANCH ** -0.5),
        "final_norm_w": 1.0 + nrm((D,), 0.02),
    }


def reference(x, c, ada_w, ada_b, norm_w,
              gdn_w_in, gdn_conv_w, gdn_a_log, gdn_dt_bias, gdn_norm_w, gdn_w_out,
              hgrn_w_in, hgrn_lb_logits, hgrn_norm_w, hgrn_w_out,
              rwkv_w_in, rwkv_mu, rwkv_w0, rwkv_w_lora_up, rwkv_a0, rwkv_a_lora_up,
              rwkv_k_k, rwkv_k_a, rwkv_r_k, rwkv_ln_w, rwkv_ln_b, rwkv_w_out,
              final_norm_w):
    c_act = jax.nn.silu(c.astype(jnp.float32))
    lb_p = jax.nn.softmax(hgrn_lb_logits.astype(jnp.float32), axis=0)
    lower_bounds = jnp.cumsum(lb_p, axis=0) - lb_p[0]
    for i in range(DEPTH):
        mod = c_act @ ada_w[i] + ada_b[i]
        shift, scale, gate = jnp.split(mod, 3, axis=-1)
        h = rms_norm(x, norm_w[i]) * (1.0 + scale[:, None, :]) + shift[:, None, :]
        j = i // N_MIXERS
        kind = i % N_MIXERS
        if kind == 0:
            y = gdn_mixer(h, gdn_w_in[j], gdn_conv_w[j], gdn_a_log[j], gdn_dt_bias[j],
                          gdn_norm_w[j], gdn_w_out[j])
        elif kind == 1:
            y = hgrn2_mixer(h, hgrn_w_in[j], lower_bounds[i], hgrn_norm_w[j], hgrn_w_out[j])
        else:
            y = rwkv7_mixer(h, rwkv_w_in[j], rwkv_mu[j], rwkv_w0[j], rwkv_w_lora_up[j],
                            rwkv_a0[j], rwkv_a_lora_up[j], rwkv_k_k[j], rwkv_k_a[j],
                            rwkv_r_k[j], rwkv_ln_w[j], rwkv_ln_b[j], rwkv_w_out[j])
        x = (x + gate[:, None, :] * y).astype(x.dtype)
    return rms_norm(x, final_norm_w).astype(x.dtype)
```

```python
import functools

import jax
import jax.numpy as jnp
from jax import lax
from jax.experimental import pallas as pl
from jax.experimental.pallas import tpu as pltpu

F32 = jnp.float32
BF16 = jnp.bfloat16
HIGHEST = lax.Precision.HIGHEST

NORM_EPS = 1e-6
L2_EPS = 1e-6
RWKV_L2_EPS = 1e-12
RWKV_GN_EPS_PER_DIM = 1e-5

LANE = 128
SUBLANE = 8
CHUNK = 64
SUB = 8
TIME_BLOCK = 256
VMEM_LIMIT = 48 * 1024 * 1024


def _dot(a, b):
    return jnp.dot(a, b, preferred_element_type=F32, precision=HIGHEST)


def _dot_nt(a, b):
    return lax.dot_general(a, b, (((1,), (1,)), ((), ())),
                           preferred_element_type=F32, precision=HIGHEST)


def _dot_tn(a, b):
    return lax.dot_general(a, b, (((0,), (0,)), ((), ())),
                           preferred_element_type=F32, precision=HIGHEST)


def _sigmoid(x):
    return 1.0 / (1.0 + jnp.exp(-x))


def _silu(x):
    return x * _sigmoid(x)


def _softplus(x):
    return jnp.maximum(x, 0.0) + jnp.log1p(jnp.exp(-jnp.abs(x)))


def _iotas(n):
    ri = lax.broadcasted_iota(jnp.int32, (n, n), 0)
    ci = lax.broadcasted_iota(jnp.int32, (n, n), 1)
    return ri, ci


def _same_block(ri, ci, b):
    shift = b.bit_length() - 1
    assert 1 << shift == b
    return (ri >> shift) == (ci >> shift)


def _unit_lower_inverse(a):
    n = a.shape[0]
    ri, ci = _iotas(n)
    eye = (ri == ci).astype(F32)
    same = _same_block(ri, ci, SUB)
    nb = jnp.where(same, a, 0.0)
    nb2 = _dot(nb, nb)
    nb4 = _dot(nb2, nb2)
    inv = _dot(_dot(eye - nb, eye + nb2), eye + nb4)
    b = SUB
    while b < n:
        same_b = _same_block(ri, ci, b)
        same_2b = _same_block(ri, ci, 2 * b)
        off = jnp.where(same_2b, jnp.where(same_b, 0.0, a), 0.0)
        inv = inv - _dot(_dot(inv, off), inv)
        b *= 2
    return inv


def _row_to_col(row, n):
    ri, ci = _iotas(n)
    return jnp.sum(jnp.where(ri == ci, row, 0.0), axis=1, keepdims=True)


def _mod_kernel(c_ref, w_ref, b_ref, o_ref):
    c = c_ref[...]
    o_ref[0] = _dot(_silu(c), w_ref[0]) + b_ref[0]


def _modulations(c, ada_w, ada_b):
    depth, d, d3 = ada_w.shape
    bsz = c.shape[0]
    rows = -(-bsz // SUBLANE) * SUBLANE
    c_pad = jnp.zeros((rows, d), F32).at[:bsz].set(c.astype(F32))
    out = pl.pallas_call(
        _mod_kernel,
        grid=(depth, d3 // d),
        in_specs=[
            pl.BlockSpec((rows, d), lambda i, j: (0, 0)),
            pl.BlockSpec((1, d, d), lambda i, j: (i, 0, j)),
            pl.BlockSpec((1, 1, d), lambda i, j: (i, 0, j)),
        ],
        out_specs=pl.BlockSpec((1, rows, d), lambda i, j: (i, 0, j)),
        out_shape=jax.ShapeDtypeStruct((depth, rows, d3), F32),
        compiler_params=pltpu.CompilerParams(
            dimension_semantics=("parallel", "parallel"), vmem_limit_bytes=VMEM_LIMIT),
        name="adaln_mod",
    )(c_pad, ada_w, ada_b.reshape(depth, 1, d3))
    return out[:, :bsz]


def _lb_kernel(x_ref, o_ref):
    x = x_ref[...]
    depth = x.shape[0]
    m = jnp.max(x, axis=0, keepdims=True)
    e = jnp.exp(x - m)
    p = e / jnp.sum(e, axis=0, keepdims=True)
    acc = jnp.zeros_like(p[0:1])
    for i in range(depth):
        acc = acc + p[i:i + 1]
        o_ref[i:i + 1, :] = acc - p[0:1]


def _lower_bounds(logits):
    return pl.pallas_call(
        _lb_kernel,
        out_shape=jax.ShapeDtypeStruct(logits.shape, F32),
        name="hgrn_lower_bounds",
    )(logits.astype(F32))


def _inproj_kernel(x_ref, nw_ref, sc_ref, sh_ref, w_ref, o_ref, h_scr):
    @pl.when(pl.program_id(2) == 0)
    def _():
        x = x_ref[0]
        ms = jnp.mean(x * x, axis=-1, keepdims=True)
        h = x * lax.rsqrt(ms + NORM_EPS) * nw_ref[...]
        h = h * (1.0 + sc_ref[0]) + sh_ref[0]
        h_scr[...] = h.astype(BF16)

    o_ref[0] = jnp.dot(h_scr[...], w_ref[...], preferred_element_type=F32)


def _col_tile(n):
    best = LANE
    for k in range(1, n // LANE + 1):
        if n % (k * LANE) == 0 and k * LANE <= 1024:
            best = k * LANE
    return best


def _in_projection(x, norm_w, scale, shift, w_in):
    bsz, t, d = x.shape
    n = w_in.shape[1]
    n_pad = -(-n // LANE) * LANE
    w = w_in.astype(BF16)
    if n_pad != n:
        w = jnp.pad(w, ((0, 0), (0, n_pad - n)))
    tm = min(512, t)
    tn = _col_tile(n_pad)
    return pl.pallas_call(
        _inproj_kernel,
        grid=(bsz, t // tm, n_pad // tn),
        in_specs=[
            pl.BlockSpec((1, tm, d), lambda b, i, j: (b, i, 0)),
            pl.BlockSpec((1, d), lambda b, i, j: (0, 0)),
            pl.BlockSpec((1, 1, d), lambda b, i, j: (b, 0, 0)),
            pl.BlockSpec((1, 1, d), lambda b, i, j: (b, 0, 0)),
            pl.BlockSpec((d, tn), lambda b, i, j: (0, j)),
        ],
        out_specs=pl.BlockSpec((1, tm, tn), lambda b, i, j: (b, i, j)),
        out_shape=jax.ShapeDtypeStruct((bsz, t, n_pad), F32),
        scratch_shapes=[pltpu.VMEM((tm, d), BF16)],
        compiler_params=pltpu.CompilerParams(
            dimension_semantics=("parallel", "parallel", "arbitrary"),
            vmem_limit_bytes=VMEM_LIMIT),
        name="norm_inproj",
    )(x, norm_w.reshape(1, d), scale.reshape(bsz, 1, d), shift.reshape(bsz, 1, d), w)


def _outproj_kernel(og_ref, w_ref, x_ref, g_ref, fw_ref, o_ref, *, final):
    y = jnp.dot(og_ref[0], w_ref[...], preferred_element_type=F32)
    xn = x_ref[0] + g_ref[0] * y
    if final:
        ms = jnp.mean(xn * xn, axis=-1, keepdims=True)
        xn = xn * lax.rsqrt(ms + NORM_EPS) * fw_ref[...]
    o_ref[0] = xn


def _out_projection(og, w_out, x, gate, final_w, final):
    bsz, t, d = x.shape
    kdim = og.shape[-1]
    tm = min(512, t)
    return pl.pallas_call(
        functools.partial(_outproj_kernel, final=final),
        grid=(bsz, t // tm),
        in_specs=[
            pl.BlockSpec((1, tm, kdim), lambda b, i: (b, i, 0)),
            pl.BlockSpec((kdim, d), lambda b, i: (0, 0)),
            pl.BlockSpec((1, tm, d), lambda b, i: (b, i, 0)),
            pl.BlockSpec((1, 1, d), lambda b, i: (b, 0, 0)),
            pl.BlockSpec((1, d), lambda b, i: (0, 0)),
        ],
        out_specs=pl.BlockSpec((1, tm, d), lambda b, i: (b, i, 0)),
        out_shape=jax.ShapeDtypeStruct((bsz, t, d), F32),
        compiler_params=pltpu.CompilerParams(
            dimension_semantics=("parallel", "parallel"), vmem_limit_bytes=VMEM_LIMIT),
        name="outproj_residual",
    )(og, w_out.astype(BF16), x, gate.reshape(bsz, 1, d), final_w.reshape(1, d))


def _fill_shift_buffer(buf, x_ref, halo_ref, t):
    buf[0:SUBLANE, :] = jnp.where(t > 0, halo_ref[0], 0.0)
    buf[SUBLANE:, :] = x_ref[0]


def _halo_spec(width, col_fn, tb):
    per = tb // SUBLANE
    return pl.BlockSpec(
        (1, SUBLANE, width),
        lambda b, h, t: (b, jnp.maximum(t * per - 1, 0), col_fn(h)))


def _gdn_kernel(alog_ref, dtb_ref,
                q_ref, k_ref, v_ref, z_ref, ba_ref,
                qh_ref, kh_ref, vh_ref,
                cq_ref, ck_ref, cv_ref, nw_ref,
                o_ref,
                s_scr, bq, bk, bv, *, heads, dk):
    h = pl.program_id(1)
    t = pl.program_id(2)
    tb = q_ref.shape[1]
    taps = cq_ref.shape[0]

    @pl.when(t == 0)
    def _():
        s_scr[...] = jnp.zeros_like(s_scr)

    def conv_silu(x_ref, halo_ref, cw_ref, buf):
        _fill_shift_buffer(buf, x_ref, halo_ref, t)
        base = SUBLANE - (taps - 1)
        acc = buf[base:base + tb, :] * cw_ref[0:1, :]
        for j in range(1, taps):
            acc = acc + buf[base + j:base + j + tb, :] * cw_ref[j:j + 1, :]
        return _silu(acc)

    q = conv_silu(q_ref, qh_ref, cq_ref, bq)
    k = conv_silu(k_ref, kh_ref, ck_ref, bk)
    v = conv_silu(v_ref, vh_ref, cv_ref, bv)
    q = q * lax.rsqrt(jnp.sum(q * q, axis=-1, keepdims=True) + L2_EPS) * (dk ** -0.5)
    k = k * lax.rsqrt(jnp.sum(k * k, axis=-1, keepdims=True) + L2_EPS)

    ba = ba_ref[0]
    lane = lax.broadcasted_iota(jnp.int32, ba.shape, 1)
    bcol = jnp.sum(jnp.where(lane == h, ba, 0.0), axis=1, keepdims=True)
    acol = jnp.sum(jnp.where(lane == h + heads, ba, 0.0), axis=1, keepdims=True)
    beta = _sigmoid(bcol)
    a_scale = jnp.exp(jnp.full((1, 1), alog_ref[h], F32))
    g = -a_scale * _softplus(acol + dtb_ref[h])

    z = z_ref[0]
    nw = nw_ref[...]
    ri, ci = _iotas(CHUNK)
    incl = ri >= ci
    strict = ri > ci
    s = s_scr[...]
    for c in range(tb // CHUNK):
        sl = slice(c * CHUNK, (c + 1) * CHUNK)
        qc, kc, vc, bc, gcol = q[sl], k[sl], v[sl], beta[sl], g[sl]
        g_row = jnp.sum(jnp.where(ri == ci, gcol, 0.0), axis=0, keepdims=True)
        gc_col = jnp.sum(jnp.where(incl, g_row, 0.0), axis=1, keepdims=True)
        gc_row = jnp.sum(jnp.where(ri <= ci, gcol, 0.0), axis=0, keepdims=True)
        decay = jnp.where(incl, jnp.exp(gc_col - gc_row), 0.0)
        kb = kc * bc
        a_mat = jnp.where(strict, _dot_nt(kb, kc) * decay, 0.0)
        tinv = _unit_lower_inverse(a_mat)
        e_gc = jnp.exp(gc_col)
        u = _dot(tinv, vc * bc)
        w = _dot(tinv, kb * e_gc)
        attn = _dot_nt(qc, kc) * decay
        g_last = gc_col[CHUNK - 1:CHUNK, :]
        k_dec = kc * jnp.exp(g_last - gc_col)
        v_new = u - _dot(w, s)
        o = _dot(qc * e_gc, s) + _dot(attn, v_new)
        s = s * jnp.exp(g_last) + _dot_tn(k_dec, v_new)
        ms = jnp.mean(o * o, axis=-1, keepdims=True)
        on = o * lax.rsqrt(ms + NORM_EPS) * nw
        o_ref[0, sl, :] = (on * _silu(z[sl])).astype(o_ref.dtype)
    s_scr[...] = s


def _gdn_mixer(p, conv_w, a_log, dt_bias, norm_w, heads, dk, dv):
    bsz, t, _ = p.shape
    tb = min(TIME_BLOCK, t)
    qk = heads * dk
    kq = qk // dk
    kv = 2 * qk // dv
    kz = (2 * qk + heads * dv) // dv
    kba = (2 * qk + 2 * heads * dv) // LANE
    taps = conv_w.shape[0]
    smem = pl.BlockSpec(memory_space=pltpu.SMEM)
    in_specs = [
        smem, smem,
        pl.BlockSpec((1, tb, dk), lambda b, h, i: (b, i, h)),
        pl.BlockSpec((1, tb, dk), lambda b, h, i: (b, i, kq + h)),
        pl.BlockSpec((1, tb, dv), lambda b, h, i: (b, i, kv + h)),
        pl.BlockSpec((1, tb, dv), lambda b, h, i: (b, i, kz + h)),
        pl.BlockSpec((1, tb, LANE), lambda b, h, i: (b, i, kba)),
        _halo_spec(dk, lambda h: h, tb),
        _halo_spec(dk, lambda h: kq + h, tb),
        _halo_spec(dv, lambda h: kv + h, tb),
        pl.BlockSpec((taps, dk), lambda b, h, i: (0, h)),
        pl.BlockSpec((taps, dk), lambda b, h, i: (0, kq + h)),
        pl.BlockSpec((taps, dv), lambda b, h, i: (0, kv + h)),
        pl.BlockSpec((1, dv), lambda b, h, i: (0, 0)),
    ]
    return pl.pallas_call(
        functools.partial(_gdn_kernel, heads=heads, dk=dk),
        grid=(bsz, heads, t // tb),
        in_specs=in_specs,
        out_specs=pl.BlockSpec((1, tb, dv), lambda b, h, i: (b, i, h)),
        out_shape=jax.ShapeDtypeStruct((bsz, t, heads * dv), BF16),
        scratch_shapes=[
            pltpu.VMEM((dk, dv), F32),
            pltpu.VMEM((tb + SUBLANE, dk), F32),
            pltpu.VMEM((tb + SUBLANE, dk), F32),
            pltpu.VMEM((tb + SUBLANE, dv), F32),
        ],
        compiler_params=pltpu.CompilerParams(
            dimension_semantics=("parallel", "parallel", "arbitrary"),
            vmem_limit_bytes=VMEM_LIMIT),
        name="gdn_mixer",
    )(a_log.astype(F32), dt_bias.astype(F32), p, p, p, p, p, p, p, p,
      conv_w, conv_w, conv_w, norm_w.reshape(1, dv))


def _hgrn_kernel(q_ref, f_ref, i_ref, z_ref, lb_ref, nw_ref, o_ref, s_scr):
    t = pl.program_id(2)
    tb = q_ref.shape[1]
    dk = q_ref.shape[2]

    @pl.when(t == 0)
    def _():
        s_scr[...] = jnp.zeros_like(s_scr)

    lb = lb_ref[...]
    q = _silu(q_ref[0])
    f = f_ref[0]
    a1 = jnp.log(lb)
    a2 = jnp.log1p(-lb) - _softplus(-f)
    log_f = jnp.maximum(a1, a2) + jnp.log1p(jnp.exp(-jnp.abs(a1 - a2)))
    k = (1.0 - lb) * _sigmoid(-f)
    v = i_ref[0]
    z = z_ref[0]
    nw = nw_ref[...]

    ri, ci = _iotas(CHUNK)
    tril = (ri >= ci).astype(F32)
    rows = lax.broadcasted_iota(jnp.int32, (CHUNK, 1), 0)
    s = s_scr[...]
    for c in range(tb // CHUNK):
        sl = slice(c * CHUNK, (c + 1) * CHUNK)
        qc, kc, vc = q[sl], k[sl], v[sl]
        gcum = _dot(tril, log_f[sl])
        g_last = gcum[CHUNK - 1:CHUNK, :]
        blocks = []
        for i in range(CHUNK // SUB):
            lo, hi = i * SUB, (i + 1) * SUB
            ref = gcum[lo - 1:lo, :] if i > 0 else jnp.zeros_like(g_last)
            qs = qc[lo:hi] * jnp.exp(gcum[lo:hi] - ref)
            ks = jnp.where(rows < hi, kc * jnp.exp(ref - gcum), 0.0)
            blocks.append(_dot_nt(qs, ks))
        attn = jnp.where(ri >= ci, jnp.concatenate(blocks, axis=0), 0.0)
        o = _dot(qc * jnp.exp(gcum), s) + _dot(attn, vc)
        k_dec = kc * jnp.exp(g_last - gcum)
        s = s * _row_to_col(jnp.exp(g_last), dk) + _dot_tn(k_dec, vc)
        ms = jnp.mean(o * o, axis=-1, keepdims=True)
        on = o * lax.rsqrt(ms + NORM_EPS) * nw
        o_ref[0, sl, :] = (on * _silu(z[sl])).astype(o_ref.dtype)
    s_scr[...] = s


def _hgrn_mixer(p, lower_bound, norm_w, heads, dk, dv):
    bsz, t, _ = p.shape
    tb = min(TIME_BLOCK, t)
    fdim = heads * dk
    kf = fdim // dk
    ki = 2 * fdim // dv
    kz = (2 * fdim + heads * dv) // dv
    return pl.pallas_call(
        _hgrn_kernel,
        grid=(bsz, heads, t // tb),
        in_specs=[
            pl.BlockSpec((1, tb, dk), lambda b, h, i: (b, i, h)),
            pl.BlockSpec((1, tb, dk), lambda b, h, i: (b, i, kf + h)),
            pl.BlockSpec((1, tb, dv), lambda b, h, i: (b, i, ki + h)),
            pl.BlockSpec((1, tb, dv), lambda b, h, i: (b, i, kz + h)),
            pl.BlockSpec((1, dk), lambda b, h, i: (0, h)),
            pl.BlockSpec((1, dv), lambda b, h, i: (0, 0)),
        ],
        out_specs=pl.BlockSpec((1, tb, dv), lambda b, h, i: (b, i, h)),
        out_shape=jax.ShapeDtypeStruct((bsz, t, heads * dv), BF16),
        scratch_shapes=[pltpu.VMEM((dk, dv), F32)],
        compiler_params=pltpu.CompilerParams(
            dimension_semantics=("parallel", "parallel", "arbitrary"),
            vmem_limit_bytes=VMEM_LIMIT),
        name="hgrn2_mixer",
    )(p, p, p, p, lower_bound.reshape(1, fdim), norm_w.reshape(1, dv))


def _rwkv_kernel(r_ref, k_ref, v_ref, z_ref, wa_ref,
                 rh_ref, kh_ref, vh_ref, zh_ref, wah_ref,
                 mu_r, mu_k, mu_v, mu_z, mu_wa,
                 w0_ref, a0_ref, kk_ref, ka_ref, rk_ref, lnw_ref, lnb_ref,
                 wl_ref, al_ref,
                 o_ref,
                 s_scr, buf, *, dh):
    t = pl.program_id(2)
    tb = r_ref.shape[1]
    lora = wl_ref.shape[0]

    @pl.when(t == 0)
    def _():
        s_scr[...] = jnp.zeros_like(s_scr)

    def shifted(x_ref, halo_ref, mu_ref):
        _fill_shift_buffer(buf, x_ref, halo_ref, t)
        cur = buf[SUBLANE:SUBLANE + tb, :]
        prev = buf[SUBLANE - 1:SUBLANE - 1 + tb, :]
        return cur + (prev - cur) * mu_ref[...]

    r = shifted(r_ref, rh_ref, mu_r)
    k = shifted(k_ref, kh_ref, mu_k)
    v = shifted(v_ref, vh_ref, mu_v)
    z = shifted(z_ref, zh_ref, mu_z)
    wa = shifted(wa_ref, wah_ref, mu_wa)
    wd = wa[:, :lora]
    ad = wa[:, lora:]
    w_log = -_softplus(-(w0_ref[...] + _dot(jnp.tanh(wd), wl_ref[...]))) - 0.5
    log_w = -jnp.exp(w_log)
    a = _sigmoid(a0_ref[...] + _dot(ad, al_ref[...]))
    kkx = k * kk_ref[...]
    k2 = k * (1.0 + (a - 1.0) * ka_ref[...])
    rkr = r * k2 * rk_ref[...]
    lnw = lnw_ref[...]
    lnb = lnb_ref[...]
    gate = _silu(z)

    ri, ci = _iotas(CHUNK)
    incl = ri >= ci
    strict = ri > ci
    tril = incl.astype(F32)
    outs = []
    for hh in range(LANE // dh):
        hs = slice(hh * dh, (hh + 1) * dh)
        kkh = kkx[:, hs]
        kkh = kkh * lax.rsqrt(jnp.sum(kkh * kkh, axis=-1, keepdims=True) + RWKV_L2_EPS)
        rh, kh, vh, ah, lwh = r[:, hs], k2[:, hs], v[:, hs], a[:, hs], log_w[:, hs]
        bonus = jnp.sum(rkr[:, hs], axis=-1, keepdims=True) * vh
        s = s_scr[hh]
        ys = []
        for c in range(tb // CHUNK):
            sl = slice(c * CHUNK, (c + 1) * CHUNK)
            rc, kc, vc, kkc, lwc = rh[sl], kh[sl], vh[sl], kkh[sl], lwh[sl]
            bc = kkc * ah[sl]
            gcum = _dot(tril, lwc)
            g_prev = gcum - lwc
            e_pos = jnp.exp(gcum)
            e_neg = jnp.exp(-gcum)
            a_t = -kkc * jnp.exp(g_prev)
            r_t = rc * e_pos
            b_t = bc * e_neg
            k_t = kc * e_neg
            a_ab = jnp.where(strict, _dot_nt(a_t, b_t), 0.0)
            a_ak = jnp.where(strict, _dot_nt(a_t, k_t), 0.0)
            a_rb = jnp.where(incl, _dot_nt(r_t, b_t), 0.0)
            a_rk = jnp.where(incl, _dot_nt(r_t, k_t), 0.0)
            tinv = _unit_lower_inverse(-a_ab)
            w1 = _dot(tinv, a_t)
            w2 = _dot(tinv, a_ak)
            u = _dot(w1, s) + _dot(w2, vc)
            y = _dot(r_t, s) + _dot(a_rb, u) + _dot(a_rk, vc)
            g_last = gcum[CHUNK - 1:CHUNK, :]
            e_last = jnp.exp(g_last - gcum)
            s = (s * _row_to_col(jnp.exp(g_last), dh)
                 + _dot_tn(bc * e_last, u) + _dot_tn(kc * e_last, vc))
            mean = jnp.mean(y, axis=-1, keepdims=True)
            var = jnp.mean(jnp.square(y - mean), axis=-1, keepdims=True)
            ys.append((y - mean) * lax.rsqrt(var + dh * RWKV_GN_EPS_PER_DIM))
        s_scr[hh] = s
        yn = jnp.concatenate(ys, axis=0)
        outs.append((yn * lnw[:, hs] + lnb[:, hs] + bonus) * gate[:, hs])
    o_ref[0] = jnp.concatenate(outs, axis=-1).astype(o_ref.dtype)


def _rwkv_mixer(p, mu, w0, w_lora_up, a0, a_lora_up, k_k, k_a, r_k, ln_w, ln_b, dh):
    bsz, t, _ = p.shape
    branch = w0.shape[0]
    lora = w_lora_up.shape[0]
    assert 2 * lora == LANE and LANE % dh == 0
    tb = min(TIME_BLOCK, t)
    nb = branch // LANE
    pairs = nb

    def stream(j):
        return pl.BlockSpec((1, tb, LANE), lambda b, h, i: (b, i, j * nb + h))

    def stream_halo(j):
        return _halo_spec(LANE, lambda h: j * nb + h, tb)

    def mu_spec(j):
        return pl.BlockSpec((1, LANE), lambda b, h, i: (0, j * nb + h))

    vec = pl.BlockSpec((1, LANE), lambda b, h, i: (0, h))
    wa_blk = 4 * nb
    in_specs = (
        [stream(j) for j in range(4)]
        + [pl.BlockSpec((1, tb, LANE), lambda b, h, i: (b, i, wa_blk))]
        + [stream_halo(j) for j in range(4)]
        + [_halo_spec(LANE, lambda h: wa_blk, tb)]
        + [mu_spec(j) for j in range(4)]
        + [pl.BlockSpec((1, LANE), lambda b, h, i: (0, wa_blk))]
        + [vec] * 7
        + [pl.BlockSpec((lora, LANE), lambda b, h, i: (0, h))] * 2
    )
    row = lambda a: a.reshape(1, -1)
    return pl.pallas_call(
        functools.partial(_rwkv_kernel, dh=dh),
        grid=(bsz, pairs, t // tb),
        in_specs=in_specs,
        out_specs=pl.BlockSpec((1, tb, LANE), lambda b, h, i: (b, i, h)),
        out_shape=jax.ShapeDtypeStruct((bsz, t, branch), BF16),
        scratch_shapes=[
            pltpu.VMEM((LANE // dh, dh, dh), F32),
            pltpu.VMEM((tb + SUBLANE, LANE), F32),
        ],
        compiler_params=pltpu.CompilerParams(
            dimension_semantics=("parallel", "parallel", "arbitrary"),
            vmem_limit_bytes=VMEM_LIMIT),
        name="rwkv7_mixer",
    )(p, p, p, p, p, p, p, p, p, p,
      row(mu), row(mu), row(mu), row(mu), row(mu),
      row(w0), row(a0), row(k_k), row(k_a), row(r_k), row(ln_w), row(ln_b),
      w_lora_up, a_lora_up)


def kernel(x, c, ada_w, ada_b, norm_w, gdn_w_in, gdn_conv_w, gdn_a_log, gdn_dt_bias, gdn_norm_w, gdn_w_out, hgrn_w_in, hgrn_lb_logits, hgrn_norm_w, hgrn_w_out, rwkv_w_in, rwkv_mu, rwkv_w0, rwkv_w_lora_up, rwkv_a0, rwkv_a_lora_up, rwkv_k_k, rwkv_k_a, rwkv_r_k, rwkv_ln_w, rwkv_ln_b, rwkv_w_out, final_norm_w):
    depth, d = norm_w.shape
    n_mixers = 3
    gdn_heads = gdn_a_log.shape[1]
    gdn_dv = gdn_norm_w.shape[1]
    gdn_dk = (gdn_conv_w.shape[2] - gdn_heads * gdn_dv) // (2 * gdn_heads)
    hgrn_dv = hgrn_norm_w.shape[1]
    hgrn_f = hgrn_lb_logits.shape[1]
    hgrn_heads = (hgrn_w_in.shape[2] - 2 * hgrn_f) // (2 * hgrn_dv)
    hgrn_dk = hgrn_f // hgrn_heads
    rwkv_dh = rwkv_r_k.shape[2]

    x = x.astype(F32)
    mod = _modulations(c, ada_w, ada_b)
    lower_bounds = _lower_bounds(hgrn_lb_logits)
    for i in range(depth):
        shift, scale, gate = mod[i, :, :d], mod[i, :, d:2 * d], mod[i, :, 2 * d:]
        j = i // n_mixers
        kind = i % n_mixers
        if kind == 0:
            p = _in_projection(x, norm_w[i], scale, shift, gdn_w_in[j])
            og = _gdn_mixer(p, gdn_conv_w[j], gdn_a_log[j], gdn_dt_bias[j], gdn_norm_w[j],
                            gdn_heads, gdn_dk, gdn_dv)
            w_out = gdn_w_out[j]
        elif kind == 1:
            p = _in_projection(x, norm_w[i], scale, shift, hgrn_w_in[j])
            og = _hgrn_mixer(p, lower_bounds[i], hgrn_norm_w[j], hgrn_heads, hgrn_dk, hgrn_dv)
            w_out = hgrn_w_out[j]
        else:
            p = _in_projection(x, norm_w[i], scale, shift, rwkv_w_in[j])
            og = _rwkv_mixer(p, rwkv_mu[j], rwkv_w0[j], rwkv_w_lora_up[j], rwkv_a0[j],
                             rwkv_a_lora_up[j], rwkv_k_k[j], rwkv_k_a[j], rwkv_r_k[j],
                             rwkv_ln_w[j], rwkv_ln_b[j], rwkv_dh)
            w_out = rwkv_w_out[j]
        x = _out_projection(og, w_out, x, gate, final_norm_w, final=(i == depth - 1))
    return x
```

```python
import functools

import jax
import jax.numpy as jnp
from jax import lax
from jax.experimental import pallas as pl
from jax.experimental.pallas import tpu as pltpu

F32 = jnp.float32
BF16 = jnp.bfloat16
HIGHEST = lax.Precision.HIGHEST

NORM_EPS = 1e-6
L2_EPS = 1e-6
RWKV_L2_EPS = 1e-12
RWKV_GN_EPS_PER_DIM = 1e-5

LANE = 128
SUBLANE = 8
CHUNK = 64
SUB = 8
TIME_BLOCK = 512
VMEM_LIMIT = 48 * 1024 * 1024


def _dot(a, b):
    return jnp.dot(a, b, preferred_element_type=F32, precision=HIGHEST)


def _dot_nt(a, b):
    return lax.dot_general(a, b, (((1,), (1,)), ((), ())),
                           preferred_element_type=F32, precision=HIGHEST)


def _dot_tn(a, b):
    return lax.dot_general(a, b, (((0,), (0,)), ((), ())),
                           preferred_element_type=F32, precision=HIGHEST)


def _mm(a, b):
    return jnp.dot(a.astype(BF16), b.astype(BF16), preferred_element_type=F32)


def _mm_nt(a, b):
    return lax.dot_general(a.astype(BF16), b.astype(BF16), (((1,), (1,)), ((), ())),
                           preferred_element_type=F32)


def _mm_tn(a, b):
    return lax.dot_general(a.astype(BF16), b.astype(BF16), (((0,), (0,)), ((), ())),
                           preferred_element_type=F32)


def _sigmoid(x):
    return 1.0 / (1.0 + jnp.exp(-x))


def _silu(x):
    return x * _sigmoid(x)


def _softplus(x):
    return jnp.maximum(x, 0.0) + jnp.log1p(jnp.exp(-jnp.abs(x)))


def _iotas(n):
    ri = lax.broadcasted_iota(jnp.int32, (n, n), 0)
    ci = lax.broadcasted_iota(jnp.int32, (n, n), 1)
    return ri, ci


def _same_block(ri, ci, b):
    shift = b.bit_length() - 1
    assert 1 << shift == b
    return (ri >> shift) == (ci >> shift)


def _unit_lower_inverses(mats, top):
    n = mats[0].shape[0]
    ri, ci = _iotas(n)
    eye = (ri == ci).astype(F32)
    same = _same_block(ri, ci, SUB)
    nb = [jnp.where(same, a, 0.0) for a in mats]
    nb2 = [_mm(x, x) for x in nb]
    nb4 = [_mm(x, x) for x in nb2]
    inv = [_mm(eye - x, eye + x2) for x, x2 in zip(nb, nb2)]
    inv = [_mm(x, eye + x4) for x, x4 in zip(inv, nb4)]
    b = SUB
    while b < top:
        same_b = _same_block(ri, ci, b)
        same_2b = _same_block(ri, ci, 2 * b)
        off = [jnp.where(same_2b, jnp.where(same_b, 0.0, a), 0.0) for a in mats]
        tmp = [_mm(x, o) for x, o in zip(inv, off)]
        inv = [x - _mm(t, x) for x, t in zip(inv, tmp)]
        b *= 2
    return inv


def _chunk_cumsum(x):
    rows, w = x.shape
    ri, ci = _iotas(rows)
    tril = jnp.where(_same_block(ri, ci, CHUNK) & (ri >= ci), 1.0, 0.0).astype(BF16)
    hi = x.astype(BF16)
    rest = x - hi.astype(F32)
    mid = rest.astype(BF16)
    lo = (rest - mid.astype(F32)).astype(BF16)
    parts = jnp.dot(tril, jnp.concatenate([hi, mid, lo], axis=1), preferred_element_type=F32)
    return parts[:, :w] + parts[:, w:2 * w] + parts[:, 2 * w:]


def _head_sum(x, dh):
    lane = lax.broadcasted_iota(jnp.int32, x.shape, 1)
    out = jnp.zeros_like(x)
    for g in range(x.shape[1] // dh):
        in_group = (lane >= g * dh) & (lane < (g + 1) * dh)
        total = jnp.sum(jnp.where(in_group, x, 0.0), axis=-1, keepdims=True)
        out = jnp.where(in_group, total, out)
    return out


def _head_rows(x, dh):
    lane = lax.broadcasted_iota(jnp.int32, x.shape, 1)
    parts = []
    for g in range(x.shape[1] // dh):
        in_group = (lane >= g * dh) & (lane < (g + 1) * dh)
        parts.append(jnp.where(in_group, x, 0.0))
    return jnp.concatenate(parts, axis=0)


def _row_to_col(row, n):
    ri, ci = _iotas(n)
    return jnp.sum(jnp.where(ri == ci, row, 0.0), axis=1, keepdims=True)


def _mod_kernel(c_ref, w_ref, b_ref, o_ref):
    c = c_ref[...]
    o_ref[0] = _dot(_silu(c), w_ref[0]) + b_ref[0]


def _modulations(c, ada_w, ada_b):
    depth, d, d3 = ada_w.shape
    bsz = c.shape[0]
    rows = -(-bsz // SUBLANE) * SUBLANE
    c_pad = jnp.zeros((rows, d), F32).at[:bsz].set(c.astype(F32))
    out = pl.pallas_call(
        _mod_kernel,
        grid=(depth, d3 // d),
        in_specs=[
            pl.BlockSpec((rows, d), lambda i, j: (0, 0)),
            pl.BlockSpec((1, d, d), lambda i, j: (i, 0, j)),
            pl.BlockSpec((1, 1, d), lambda i, j: (i, 0, j)),
        ],
        out_specs=pl.BlockSpec((1, rows, d), lambda i, j: (i, 0, j)),
        out_shape=jax.ShapeDtypeStruct((depth, rows, d3), F32),
        compiler_params=pltpu.CompilerParams(
            dimension_semantics=("parallel", "parallel"), vmem_limit_bytes=VMEM_LIMIT),
        name="adaln_mod",
    )(c_pad, ada_w, ada_b.reshape(depth, 1, d3))
    return out[:, :bsz]


def _lb_kernel(x_ref, o_ref):
    x = x_ref[...]
    depth = x.shape[0]
    m = jnp.max(x, axis=0, keepdims=True)
    e = jnp.exp(x - m)
    p = e / jnp.sum(e, axis=0, keepdims=True)
    acc = jnp.zeros_like(p[0:1])
    for i in range(depth):
        acc = acc + p[i:i + 1]
        o_ref[i:i + 1, :] = acc - p[0:1]


def _lower_bounds(logits):
    return pl.pallas_call(
        _lb_kernel,
        out_shape=jax.ShapeDtypeStruct(logits.shape, F32),
        name="hgrn_lower_bounds",
    )(logits.astype(F32))


def _inproj_kernel(x_ref, nw_ref, sc_ref, sh_ref, w_ref, o_ref, h_scr):
    @pl.when(pl.program_id(2) == 0)
    def _():
        x = x_ref[0]
        ms = jnp.mean(x * x, axis=-1, keepdims=True)
        h = x * lax.rsqrt(ms + NORM_EPS) * nw_ref[...]
        h = h * (1.0 + sc_ref[0]) + sh_ref[0]
        h_scr[...] = h.astype(BF16)

    o_ref[0] = jnp.dot(h_scr[...], w_ref[...], preferred_element_type=F32)


def _col_tile(n):
    best = LANE
    for k in range(1, n // LANE + 1):
        if n % (k * LANE) == 0 and k * LANE <= 1024:
            best = k * LANE
    return best


def _in_projection(x, norm_w, scale, shift, w_in):
    bsz, t, d = x.shape
    n = w_in.shape[1]
    n_pad = -(-n // LANE) * LANE
    w = w_in.astype(BF16)
    if n_pad != n:
        w = jnp.pad(w, ((0, 0), (0, n_pad - n)))
    tm = min(512, t)
    tn = _col_tile(n_pad)
    return pl.pallas_call(
        _inproj_kernel,
        grid=(bsz, t // tm, n_pad // tn),
        in_specs=[
            pl.BlockSpec((1, tm, d), lambda b, i, j: (b, i, 0)),
            pl.BlockSpec((1, d), lambda b, i, j: (0, 0)),
            pl.BlockSpec((1, 1, d), lambda b, i, j: (b, 0, 0)),
            pl.BlockSpec((1, 1, d), lambda b, i, j: (b, 0, 0)),
            pl.BlockSpec((d, tn), lambda b, i, j: (0, j)),
        ],
        out_specs=pl.BlockSpec((1, tm, tn), lambda b, i, j: (b, i, j)),
        out_shape=jax.ShapeDtypeStruct((bsz, t, n_pad), F32),
        scratch_shapes=[pltpu.VMEM((tm, d), BF16)],
        compiler_params=pltpu.CompilerParams(
            dimension_semantics=("parallel", "parallel", "arbitrary"),
            vmem_limit_bytes=VMEM_LIMIT),
        name="norm_inproj",
    )(x, norm_w.reshape(1, d), scale.reshape(bsz, 1, d), shift.reshape(bsz, 1, d), w)


def _outproj_kernel(og_ref, w_ref, x_ref, g_ref, fw_ref, o_ref, *, final):
    y = jnp.dot(og_ref[0], w_ref[...], preferred_element_type=F32)
    xn = x_ref[0] + g_ref[0] * y
    if final:
        ms = jnp.mean(xn * xn, axis=-1, keepdims=True)
        xn = xn * lax.rsqrt(ms + NORM_EPS) * fw_ref[...]
    o_ref[0] = xn


def _out_projection(og, w_out, x, gate, final_w, final):
    bsz, t, d = x.shape
    kdim = og.shape[-1]
    tm = min(512, t)
    return pl.pallas_call(
        functools.partial(_outproj_kernel, final=final),
        grid=(bsz, t // tm),
        in_specs=[
            pl.BlockSpec((1, tm, kdim), lambda b, i: (b, i, 0)),
            pl.BlockSpec((kdim, d), lambda b, i: (0, 0)),
            pl.BlockSpec((1, tm, d), lambda b, i: (b, i, 0)),
            pl.BlockSpec((1, 1, d), lambda b, i: (b, 0, 0)),
            pl.BlockSpec((1, d), lambda b, i: (0, 0)),
        ],
        out_specs=pl.BlockSpec((1, tm, d), lambda b, i: (b, i, 0)),
        out_shape=jax.ShapeDtypeStruct((bsz, t, d), F32),
        compiler_params=pltpu.CompilerParams(
            dimension_semantics=("parallel", "parallel"), vmem_limit_bytes=VMEM_LIMIT),
        name="outproj_residual",
    )(og, w_out.astype(BF16), x, gate.reshape(bsz, 1, d), final_w.reshape(1, d))


def _fill_shift_buffer(buf, x_ref, halo_ref, t):
    buf[0:SUBLANE, :] = jnp.where(t > 0, halo_ref[0], 0.0)
    buf[SUBLANE:, :] = x_ref[0]


def _halo_spec(width, col_fn, tb):
    per = tb // SUBLANE
    return pl.BlockSpec(
        (1, SUBLANE, width),
        lambda b, h, t: (b, jnp.maximum(t * per - 1, 0), col_fn(h)))


def _gdn_kernel(alog_ref, dtb_ref,
                q_ref, k_ref, v_ref, z_ref, ba_ref,
                qh_ref, kh_ref, vh_ref,
                cq_ref, ck_ref, cv_ref, nw_ref,
                o_ref,
                s_scr, bq, bk, bv, *, heads, dk):
    h = pl.program_id(1)
    t = pl.program_id(2)
    tb = q_ref.shape[1]
    taps = cq_ref.shape[0]

    @pl.when(t == 0)
    def _():
        s_scr[...] = jnp.zeros_like(s_scr)

    def conv_silu(x_ref, halo_ref, cw_ref, buf):
        _fill_shift_buffer(buf, x_ref, halo_ref, t)
        base = SUBLANE - (taps - 1)
        acc = buf[base:base + tb, :] * cw_ref[0:1, :]
        for j in range(1, taps):
            acc = acc + buf[base + j:base + j + tb, :] * cw_ref[j:j + 1, :]
        return _silu(acc)

    q = conv_silu(q_ref, qh_ref, cq_ref, bq)
    k = conv_silu(k_ref, kh_ref, ck_ref, bk)
    v = conv_silu(v_ref, vh_ref, cv_ref, bv)
    q = q * lax.rsqrt(jnp.sum(q * q, axis=-1, keepdims=True) + L2_EPS) * (dk ** -0.5)
    k = k * lax.rsqrt(jnp.sum(k * k, axis=-1, keepdims=True) + L2_EPS)

    ba = ba_ref[0]
    lane = lax.broadcasted_iota(jnp.int32, ba.shape, 1)
    bcol = jnp.sum(jnp.where(lane == h, ba, 0.0), axis=1, keepdims=True)
    acol = jnp.sum(jnp.where(lane == h + heads, ba, 0.0), axis=1, keepdims=True)
    beta = _sigmoid(bcol)
    a_scale = jnp.exp(jnp.full((1, 1), alog_ref[h], F32))
    g = -a_scale * _softplus(acol + dtb_ref[h])

    z = z_ref[0]
    nw = nw_ref[...]
    ri, ci = _iotas(CHUNK)
    incl = ri >= ci
    strict = ri > ci
    chunks = [slice(c * CHUNK, (c + 1) * CHUNK) for c in range(tb // CHUNK)]
    gc_col, decay = [], []
    for sl in chunks:
        gcol = g[sl]
        g_row = jnp.sum(jnp.where(ri == ci, gcol, 0.0), axis=0, keepdims=True)
        gc_col.append(jnp.sum(jnp.where(incl, g_row, 0.0), axis=1, keepdims=True))
        gc_row = jnp.sum(jnp.where(ri <= ci, gcol, 0.0), axis=0, keepdims=True)
        decay.append(jnp.where(incl, jnp.exp(gc_col[-1] - gc_row), 0.0))
    kb = [k[sl] * beta[sl] for sl in chunks]
    kq = [_mm_nt(jnp.concatenate([x, q[sl]], axis=0), k[sl]) for x, sl in zip(kb, chunks)]
    tinv = _unit_lower_inverses(
        [jnp.where(strict, m[:CHUNK] * d, 0.0) for m, d in zip(kq, decay)], CHUNK)
    attn = [m[CHUNK:] * d for m, d in zip(kq, decay)]
    e_gc = [jnp.exp(x) for x in gc_col]
    uw = [_mm(ti, jnp.concatenate([v[sl] * beta[sl], x * e], axis=1))
          for ti, sl, x, e in zip(tinv, chunks, kb, e_gc)]
    g_last = [x[CHUNK - 1:CHUNK, :] for x in gc_col]
    k_dec = [k[sl] * jnp.exp(gl - x) for sl, gl, x in zip(chunks, g_last, gc_col)]
    q_dec = [q[sl] * e for sl, e in zip(chunks, e_gc)]

    dv = v.shape[1]
    s = s_scr[...]
    for c, sl in enumerate(chunks):
        ws = _mm(jnp.concatenate([uw[c][:, dv:], q_dec[c]], axis=0), s)
        v_new = uw[c][:, :dv] - ws[:CHUNK]
        o = ws[CHUNK:] + _mm(attn[c], v_new)
        s = s * jnp.exp(g_last[c]) + _mm_tn(k_dec[c], v_new)
        ms = jnp.mean(o * o, axis=-1, keepdims=True)
        on = o * lax.rsqrt(ms + NORM_EPS) * nw
        o_ref[0, sl, :] = (on * _silu(z[sl])).astype(o_ref.dtype)
    s_scr[...] = s


def _gdn_mixer(p, conv_w, a_log, dt_bias, norm_w, heads, dk, dv):
    bsz, t, _ = p.shape
    tb = min(TIME_BLOCK, t)
    qk = heads * dk
    kq = qk // dk
    kv = 2 * qk // dv
    kz = (2 * qk + heads * dv) // dv
    kba = (2 * qk + 2 * heads * dv) // LANE
    taps = conv_w.shape[0]
    smem = pl.BlockSpec(memory_space=pltpu.SMEM)
    in_specs = [
        smem, smem,
        pl.BlockSpec((1, tb, dk), lambda b, h, i: (b, i, h)),
        pl.BlockSpec((1, tb, dk), lambda b, h, i: (b, i, kq + h)),
        pl.BlockSpec((1, tb, dv), lambda b, h, i: (b, i, kv + h)),
        pl.BlockSpec((1, tb, dv), lambda b, h, i: (b, i, kz + h)),
        pl.BlockSpec((1, tb, LANE), lambda b, h, i: (b, i, kba)),
        _halo_spec(dk, lambda h: h, tb),
        _halo_spec(dk, lambda h: kq + h, tb),
        _halo_spec(dv, lambda h: kv + h, tb),
        pl.BlockSpec((taps, dk), lambda b, h, i: (0, h)),
        pl.BlockSpec((taps, dk), lambda b, h, i: (0, kq + h)),
        pl.BlockSpec((taps, dv), lambda b, h, i: (0, kv + h)),
        pl.BlockSpec((1, dv), lambda b, h, i: (0, 0)),
    ]
    return pl.pallas_call(
        functools.partial(_gdn_kernel, heads=heads, dk=dk),
        grid=(bsz, heads, t // tb),
        in_specs=in_specs,
        out_specs=pl.BlockSpec((1, tb, dv), lambda b, h, i: (b, i, h)),
        out_shape=jax.ShapeDtypeStruct((bsz, t, heads * dv), BF16),
        scratch_shapes=[
            pltpu.VMEM((dk, dv), F32),
            pltpu.VMEM((tb + SUBLANE, dk), F32),
            pltpu.VMEM((tb + SUBLANE, dk), F32),
            pltpu.VMEM((tb + SUBLANE, dv), F32),
        ],
        compiler_params=pltpu.CompilerParams(
            dimension_semantics=("parallel", "parallel", "arbitrary"),
            vmem_limit_bytes=VMEM_LIMIT),
        name="gdn_mixer",
    )(a_log.astype(F32), dt_bias.astype(F32), p, p, p, p, p, p, p, p,
      conv_w, conv_w, conv_w, norm_w.reshape(1, dv))


def _hgrn_kernel(q_ref, f_ref, i_ref, z_ref, lb_ref, nw_ref, o_ref, s_scr):
    t = pl.program_id(2)
    tb = q_ref.shape[1]
    dk = q_ref.shape[2]

    @pl.when(t == 0)
    def _():
        s_scr[...] = jnp.zeros_like(s_scr)

    lb = lb_ref[...]
    q = _silu(q_ref[0])
    f = f_ref[0]
    a1 = jnp.log(lb)
    a2 = jnp.log1p(-lb) - _softplus(-f)
    log_f = jnp.maximum(a1, a2) + jnp.log1p(jnp.exp(-jnp.abs(a1 - a2)))
    k = (1.0 - lb) * _sigmoid(-f)
    v = i_ref[0]
    z = z_ref[0]
    nw = nw_ref[...]

    ri, ci = _iotas(CHUNK)
    rows = lax.broadcasted_iota(jnp.int32, (CHUNK, 1), 0)
    gcum_all = _chunk_cumsum(log_f)
    chunks = [slice(c * CHUNK, (c + 1) * CHUNK) for c in range(tb // CHUNK)]
    o_intra, s_add, s_mul, q_dec = [], [], [], []
    for sl in chunks:
        qc, kc, vc = q[sl], k[sl], v[sl]
        gcum = gcum_all[sl]
        g_last = gcum[CHUNK - 1:CHUNK, :]
        blocks = []
        for i in range(CHUNK // SUB):
            lo, hi = i * SUB, (i + 1) * SUB
            ref = gcum[lo - 1:lo, :] if i > 0 else jnp.zeros_like(g_last)
            qs = qc[lo:hi] * jnp.exp(gcum[lo:hi] - ref)
            ks = jnp.where(rows < hi, kc * jnp.exp(ref - gcum), 0.0)
            blocks.append(_mm_nt(qs, ks))
        attn = jnp.where(ri >= ci, jnp.concatenate(blocks, axis=0), 0.0)
        o_intra.append(_mm(attn, vc))
        s_add.append(_mm_tn(kc * jnp.exp(g_last - gcum), vc))
        s_mul.append(_row_to_col(jnp.exp(g_last), dk))
        q_dec.append(qc * jnp.exp(gcum))

    s = s_scr[...]
    for c, sl in enumerate(chunks):
        o = _mm(q_dec[c], s) + o_intra[c]
        s = s * s_mul[c] + s_add[c]
        ms = jnp.mean(o * o, axis=-1, keepdims=True)
        on = o * lax.rsqrt(ms + NORM_EPS) * nw
        o_ref[0, sl, :] = (on * _silu(z[sl])).astype(o_ref.dtype)
    s_scr[...] = s


def _hgrn_mixer(p, lower_bound, norm_w, heads, dk, dv):
    bsz, t, _ = p.shape
    tb = min(TIME_BLOCK, t)
    fdim = heads * dk
    kf = fdim // dk
    ki = 2 * fdim // dv
    kz = (2 * fdim + heads * dv) // dv
    return pl.pallas_call(
        _hgrn_kernel,
        grid=(bsz, heads, t // tb),
        in_specs=[
            pl.BlockSpec((1, tb, dk), lambda b, h, i: (b, i, h)),
            pl.BlockSpec((1, tb, dk), lambda b, h, i: (b, i, kf + h)),
            pl.BlockSpec((1, tb, dv), lambda b, h, i: (b, i, ki + h)),
            pl.BlockSpec((1, tb, dv), lambda b, h, i: (b, i, kz + h)),
            pl.BlockSpec((1, dk), lambda b, h, i: (0, h)),
            pl.BlockSpec((1, dv), lambda b, h, i: (0, 0)),
        ],
        out_specs=pl.BlockSpec((1, tb, dv), lambda b, h, i: (b, i, h)),
        out_shape=jax.ShapeDtypeStruct((bsz, t, heads * dv), BF16),
        scratch_shapes=[pltpu.VMEM((dk, dv), F32)],
        compiler_params=pltpu.CompilerParams(
            dimension_semantics=("parallel", "parallel", "arbitrary"),
            vmem_limit_bytes=VMEM_LIMIT),
        name="hgrn2_mixer",
    )(p, p, p, p, lower_bound.reshape(1, fdim), norm_w.reshape(1, dv))


def _rwkv_kernel(r_ref, k_ref, v_ref, z_ref, wa_ref,
                 rh_ref, kh_ref, vh_ref, zh_ref, wah_ref,
                 mu_r, mu_k, mu_v, mu_z, mu_wa,
                 w0_ref, a0_ref, kk_ref, ka_ref, rk_ref, lnw_ref, lnb_ref,
                 wl_ref, al_ref,
                 o_ref,
                 s_scr, buf, *, dh):
    t = pl.program_id(2)
    tb = r_ref.shape[1]
    lora = wl_ref.shape[0]

    @pl.when(t == 0)
    def _():
        s_scr[...] = jnp.zeros_like(s_scr)

    def shifted(x_ref, halo_ref, mu_ref):
        _fill_shift_buffer(buf, x_ref, halo_ref, t)
        cur = buf[SUBLANE:SUBLANE + tb, :]
        prev = buf[SUBLANE - 1:SUBLANE - 1 + tb, :]
        return cur + (prev - cur) * mu_ref[...]

    r = shifted(r_ref, rh_ref, mu_r)
    k = shifted(k_ref, kh_ref, mu_k)
    v = shifted(v_ref, vh_ref, mu_v)
    z = shifted(z_ref, zh_ref, mu_z)
    wa = shifted(wa_ref, wah_ref, mu_wa)
    wd = wa[:, :lora]
    ad = wa[:, lora:]
    w_log = -_softplus(-(w0_ref[...] + _mm(jnp.tanh(wd), wl_ref[...]))) - 0.5
    log_w = -jnp.exp(w_log)
    a = _sigmoid(a0_ref[...] + _mm(ad, al_ref[...]))
    k2 = k * (1.0 + (a - 1.0) * ka_ref[...])
    kkx = k * kk_ref[...]
    kk = kkx * lax.rsqrt(_head_sum(kkx * kkx, dh) + RWKV_L2_EPS)
    bonus = _head_sum(r * k2 * rk_ref[...], dh) * v
    kka = kk * a

    gcum = _chunk_cumsum(log_w)
    e_neg = jnp.exp(-gcum)
    a_t = -kk * jnp.exp(gcum - log_w)
    r_t = r * jnp.exp(gcum)
    b_t = kka * e_neg
    k_t = k2 * e_neg

    heads = LANE // dh
    n = heads * CHUNK
    ri, ci = _iotas(n)
    same_head = _same_block(ri, ci, CHUNK)
    strict = same_head & (ri > ci)
    incl = same_head & (ri >= ci)
    chunks = [slice(c * CHUNK, (c + 1) * CHUNK) for c in range(tb // CHUNK)]
    a_bd = [_head_rows(a_t[sl], dh) for sl in chunks]
    r_bd = [_head_rows(r_t[sl], dh) for sl in chunks]
    v_bd = [_head_rows(v[sl], dh) for sl in chunks]
    pm = [_mm_nt(jnp.concatenate([a_bd[c], r_bd[c]], axis=0),
                 jnp.concatenate([b_t[sl]] * heads + [k_t[sl]] * heads, axis=0))
          for c, sl in enumerate(chunks)]
    tinv = _unit_lower_inverses([jnp.where(strict, -m[:n, :n], 0.0) for m in pm], CHUNK)
    a_krk = [jnp.concatenate([jnp.where(strict, m[:n, n:], 0.0),
                              jnp.where(incl, m[n:, n:], 0.0)], axis=0) for m in pm]
    a_rb = [jnp.where(incl, m[n:, :n], 0.0) for m in pm]
    av = [_mm(x, vb) for x, vb in zip(a_krk, v_bd)]
    tw = [_mm(ti, jnp.concatenate([ab, x[:n]], axis=1)) for ti, ab, x in zip(tinv, a_bd, av)]
    g_last = [gcum[sl.stop - 1:sl.stop, :] for sl in chunks]
    e_last = [jnp.exp(gl - gcum[sl]) for gl, sl in zip(g_last, chunks)]
    b_dec = [_head_rows(kka[sl] * e, dh) for sl, e in zip(chunks, e_last)]
    s_add = [_mm_tn(_head_rows(k2[sl] * e, dh), vb) for sl, e, vb in zip(chunks, e_last, v_bd)]
    s_mul = [_row_to_col(jnp.exp(gl), LANE) for gl in g_last]

    s = s_scr[...]
    ys = []
    for c in range(len(chunks)):
        rs = _mm(jnp.concatenate([tw[c][:, :LANE], r_bd[c]], axis=0), s)
        u = rs[:n] + tw[c][:, LANE:]
        y_bd = rs[n:] + _mm(a_rb[c], u) + av[c][n:]
        s = s * s_mul[c] + _mm_tn(b_dec[c], u) + s_add[c]
        y = y_bd[:CHUNK]
        for hh in range(1, heads):
            y = y + y_bd[hh * CHUNK:(hh + 1) * CHUNK]
        ys.append(y)
    s_scr[...] = s
    y = jnp.concatenate(ys, axis=0)
    mean = _head_sum(y, dh) * (1.0 / dh)
    yc = y - mean
    var = _head_sum(yc * yc, dh) * (1.0 / dh)
    yn = yc * lax.rsqrt(var + dh * RWKV_GN_EPS_PER_DIM)
    out = (yn * lnw_ref[...] + lnb_ref[...] + bonus) * _silu(z)
    o_ref[0] = out.astype(o_ref.dtype)


def _rwkv_mixer(p, mu, w0, w_lora_up, a0, a_lora_up, k_k, k_a, r_k, ln_w, ln_b, dh):
    bsz, t, _ = p.shape
    branch = w0.shape[0]
    lora = w_lora_up.shape[0]
    assert 2 * lora == LANE and LANE % dh == 0
    tb = min(TIME_BLOCK, t)
    nb = branch // LANE
    pairs = nb

    def stream(j):
        return pl.BlockSpec((1, tb, LANE), lambda b, h, i: (b, i, j * nb + h))

    def stream_halo(j):
        return _halo_spec(LANE, lambda h: j * nb + h, tb)

    def mu_spec(j):
        return pl.BlockSpec((1, LANE), lambda b, h, i: (0, j * nb + h))

    vec = pl.BlockSpec((1, LANE), lambda b, h, i: (0, h))
    wa_blk = 4 * nb
    in_specs = (
        [stream(j) for j in range(4)]
        + [pl.BlockSpec((1, tb, LANE), lambda b, h, i: (b, i, wa_blk))]
        + [stream_halo(j) for j in range(4)]
        + [_halo_spec(LANE, lambda h: wa_blk, tb)]
        + [mu_spec(j) for j in range(4)]
        + [pl.BlockSpec((1, LANE), lambda b, h, i: (0, wa_blk))]
        + [vec] * 7
        + [pl.BlockSpec((lora, LANE), lambda b, h, i: (0, h))] * 2
    )
    row = lambda a: a.reshape(1, -1)
    return pl.pallas_call(
        functools.partial(_rwkv_kernel, dh=dh),
        grid=(bsz, pairs, t // tb),
        in_specs=in_specs,
        out_specs=pl.BlockSpec((1, tb, LANE), lambda b, h, i: (b, i, h)),
        out_shape=jax.ShapeDtypeStruct((bsz, t, branch), BF16),
        scratch_shapes=[
            pltpu.VMEM((LANE, LANE), F32),
            pltpu.VMEM((tb + SUBLANE, LANE), F32),
        ],
        compiler_params=pltpu.CompilerParams(
            dimension_semantics=("parallel", "parallel", "arbitrary"),
            vmem_limit_bytes=VMEM_LIMIT),
        name="rwkv7_mixer",
    )(p, p, p, p, p, p, p, p, p, p,
      row(mu), row(mu), row(mu), row(mu), row(mu),
      row(w0), row(a0), row(k_k), row(k_a), row(r_k), row(ln_w), row(ln_b),
      w_lora_up, a_lora_up)


def kernel(x, c, ada_w, ada_b, norm_w, gdn_w_in, gdn_conv_w, gdn_a_log, gdn_dt_bias, gdn_norm_w, gdn_w_out, hgrn_w_in, hgrn_lb_logits, hgrn_norm_w, hgrn_w_out, rwkv_w_in, rwkv_mu, rwkv_w0, rwkv_w_lora_up, rwkv_a0, rwkv_a_lora_up, rwkv_k_k, rwkv_k_a, rwkv_r_k, rwkv_ln_w, rwkv_ln_b, rwkv_w_out, final_norm_w):
    depth, d = norm_w.shape
    n_mixers = 3
    gdn_heads = gdn_a_log.shape[1]
    gdn_dv = gdn_norm_w.shape[1]
    gdn_dk = (gdn_conv_w.shape[2] - gdn_heads * gdn_dv) // (2 * gdn_heads)
    hgrn_dv = hgrn_norm_w.shape[1]
    hgrn_f = hgrn_lb_logits.shape[1]
    hgrn_heads = (hgrn_w_in.shape[2] - 2 * hgrn_f) // (2 * hgrn_dv)
    hgrn_dk = hgrn_f // hgrn_heads
    rwkv_dh = rwkv_r_k.shape[2]

    x = x.astype(F32)
    mod = _modulations(c, ada_w, ada_b)
    lower_bounds = _lower_bounds(hgrn_lb_logits)
    for i in range(depth):
        shift, scale, gate = mod[i, :, :d], mod[i, :, d:2 * d], mod[i, :, 2 * d:]
        j = i // n_mixers
        kind = i % n_mixers
        if kind == 0:
            p = _in_projection(x, norm_w[i], scale, shift, gdn_w_in[j])
            og = _gdn_mixer(p, gdn_conv_w[j], gdn_a_log[j], gdn_dt_bias[j], gdn_norm_w[j],
                            gdn_heads, gdn_dk, gdn_dv)
            w_out = gdn_w_out[j]
        elif kind == 1:
            p = _in_projection(x, norm_w[i], scale, shift, hgrn_w_in[j])
            og = _hgrn_mixer(p, lower_bounds[i], hgrn_norm_w[j], hgrn_heads, hgrn_dk, hgrn_dv)
            w_out = hgrn_w_out[j]
        else:
            p = _in_projection(x, norm_w[i], scale, shift, rwkv_w_in[j])
            og = _rwkv_mixer(p, rwkv_mu[j], rwkv_w0[j], rwkv_w_lora_up[j], rwkv_a0[j],
                             rwkv_a_lora_up[j], rwkv_k_k[j], rwkv_k_a[j], rwkv_r_k[j],
                             rwkv_ln_w[j], rwkv_ln_b[j], rwkv_dh)
            w_out = rwkv_w_out[j]
        x = _out_projection(og, w_out, x, gate, final_norm_w, final=(i == depth - 1))
    return x
```

```python
import functools

import jax
import jax.numpy as jnp
from jax import lax
from jax.experimental import pallas as pl
from jax.experimental.pallas import tpu as pltpu

F32 = jnp.float32
BF16 = jnp.bfloat16
HIGHEST = lax.Precision.HIGHEST

NORM_EPS = 1e-6
L2_EPS = 1e-6
RWKV_L2_EPS = 1e-12
RWKV_GN_EPS_PER_DIM = 1e-5

LANE = 128
SUBLANE = 8
HALO_ROWS = 16
MXU_COLS = 256
ROW_TILE = 1024
MAX_COL_TILE = 3072
CHUNK = 64
SUB = 8
TIME_BLOCK = 512
VMEM_LIMIT = 48 * 1024 * 1024


def _dot(a, b):
    return jnp.dot(a, b, preferred_element_type=F32, precision=HIGHEST)


def _mm(a, b):
    return jnp.dot(a.astype(BF16), b.astype(BF16), preferred_element_type=F32)


def _mm_nt(a, b):
    return lax.dot_general(a.astype(BF16), b.astype(BF16), (((1,), (1,)), ((), ())),
                           preferred_element_type=F32)


def _mm_tn(a, b):
    return lax.dot_general(a.astype(BF16), b.astype(BF16), (((0,), (0,)), ((), ())),
                           preferred_element_type=F32)


def _sigmoid(x):
    return 1.0 / (1.0 + jnp.exp(-x))


def _silu(x):
    return x * _sigmoid(x)


def _softplus(x):
    return jnp.maximum(x, 0.0) + jnp.log1p(jnp.exp(-jnp.abs(x)))


def _iotas(n):
    ri = lax.broadcasted_iota(jnp.int32, (n, n), 0)
    ci = lax.broadcasted_iota(jnp.int32, (n, n), 1)
    return ri, ci


def _same_block(ri, ci, b):
    shift = b.bit_length() - 1
    assert 1 << shift == b
    return (ri >> shift) == (ci >> shift)


def _unit_lower_inverses(mats, top):
    n = mats[0].shape[0]
    ri, ci = _iotas(n)
    eye = (ri == ci).astype(F32)
    same = _same_block(ri, ci, SUB)
    nb = [jnp.where(same, a, 0.0) for a in mats]
    nb2 = [_mm(x, x) for x in nb]
    yield
    nb4 = [_mm(x, x) for x in nb2]
    inv = [_mm(eye - x, eye + x2) for x, x2 in zip(nb, nb2)]
    yield
    inv = [_mm(x, eye + x4) for x, x4 in zip(inv, nb4)]
    yield
    b = SUB
    while b < top:
        same_b = _same_block(ri, ci, b)
        same_2b = _same_block(ri, ci, 2 * b)
        off = [jnp.where(same_2b, jnp.where(same_b, 0.0, a), 0.0) for a in mats]
        tmp = [_mm(x, o) for x, o in zip(inv, off)]
        yield
        inv = [x - _mm(t, x) for x, t in zip(inv, tmp)]
        yield
        b *= 2
    return inv


def _alternate(*gens):
    live = list(gens)
    while live:
        for g in list(live):
            try:
                next(g)
            except StopIteration:
                live.remove(g)


def _chunk_cumsum(x):
    rows, w = x.shape
    ri, ci = _iotas(rows)
    tril = jnp.where(_same_block(ri, ci, CHUNK) & (ri >= ci), 1.0, 0.0).astype(BF16)
    hi = x.astype(BF16)
    rest = x - hi.astype(F32)
    mid = rest.astype(BF16)
    lo = (rest - mid.astype(F32)).astype(BF16)
    parts = jnp.dot(tril, jnp.concatenate([hi, mid, lo], axis=1), preferred_element_type=F32)
    return parts[:, :w] + parts[:, w:2 * w] + parts[:, 2 * w:]


def _head_sum(x, dh):
    lane = lax.broadcasted_iota(jnp.int32, x.shape, 1)
    out = jnp.zeros_like(x)
    for g in range(x.shape[1] // dh):
        in_group = (lane >= g * dh) & (lane < (g + 1) * dh)
        total = jnp.sum(jnp.where(in_group, x, 0.0), axis=-1, keepdims=True)
        out = jnp.where(in_group, total, out)
    return out


def _head_rows(x, dh):
    lane = lax.broadcasted_iota(jnp.int32, x.shape, 1)
    parts = []
    for g in range(x.shape[1] // dh):
        in_group = (lane >= g * dh) & (lane < (g + 1) * dh)
        parts.append(jnp.where(in_group, x, 0.0))
    return jnp.concatenate(parts, axis=0)


def _row_to_col(row, n):
    ri, ci = _iotas(n)
    return jnp.sum(jnp.where(ri == ci, row, 0.0), axis=1, keepdims=True)


def _mod_kernel(c_ref, w_ref, b_ref, o_ref):
    c = c_ref[...]
    o_ref[0] = _dot(_silu(c), w_ref[0]) + b_ref[0]


def _modulations(c, ada_w, ada_b):
    depth, d, d3 = ada_w.shape
    bsz = c.shape[0]
    rows = -(-bsz // SUBLANE) * SUBLANE
    c_pad = jnp.zeros((rows, d), F32).at[:bsz].set(c.astype(F32))
    out = pl.pallas_call(
        _mod_kernel,
        grid=(depth, d3 // d),
        in_specs=[
            pl.BlockSpec((rows, d), lambda i, j: (0, 0)),
            pl.BlockSpec((1, d, d), lambda i, j: (i, 0, j)),
            pl.BlockSpec((1, 1, d), lambda i, j: (i, 0, j)),
        ],
        out_specs=pl.BlockSpec((1, rows, d), lambda i, j: (i, 0, j)),
        out_shape=jax.ShapeDtypeStruct((depth, rows, d3), F32),
        compiler_params=pltpu.CompilerParams(
            dimension_semantics=("parallel", "parallel"), vmem_limit_bytes=VMEM_LIMIT),
        name="adaln_mod",
    )(c_pad, ada_w, ada_b.reshape(depth, 1, d3))
    return out[:, :bsz]


def _lb_kernel(x_ref, o_ref):
    x = x_ref[...]
    depth = x.shape[0]
    m = jnp.max(x, axis=0, keepdims=True)
    e = jnp.exp(x - m)
    p = e / jnp.sum(e, axis=0, keepdims=True)
    acc = jnp.zeros_like(p[0:1])
    for i in range(depth):
        acc = acc + p[i:i + 1]
        o_ref[i:i + 1, :] = acc - p[0:1]


def _lower_bounds(logits):
    return pl.pallas_call(
        _lb_kernel,
        out_shape=jax.ShapeDtypeStruct(logits.shape, F32),
        name="hgrn_lower_bounds",
    )(logits.astype(F32))


def _inproj_kernel(x_ref, nw_ref, sc_ref, sh_ref, w_ref, o_ref, h_scr):
    @pl.when(pl.program_id(2) == 0)
    def _():
        x = x_ref[0]
        ms = jnp.mean(x * x, axis=-1, keepdims=True)
        h = x * lax.rsqrt(ms + NORM_EPS) * nw_ref[...]
        h = h * (1.0 + sc_ref[0]) + sh_ref[0]
        h_scr[...] = h.astype(BF16)

    o_ref[0] = jnp.dot(h_scr[...], w_ref[...], preferred_element_type=F32).astype(o_ref.dtype)


def _col_tile(n):
    best = MXU_COLS
    for k in range(1, n // MXU_COLS + 1):
        if n % (k * MXU_COLS) == 0 and k * MXU_COLS <= MAX_COL_TILE:
            best = k * MXU_COLS
    return best


def _in_projection(x, norm_w, scale, shift, w_in):
    bsz, t, d = x.shape
    n = w_in.shape[1]
    n_pad = -(-n // MXU_COLS) * MXU_COLS
    w = w_in.astype(BF16)
    if n_pad != n:
        w = jnp.pad(w, ((0, 0), (0, n_pad - n)))
    tm = min(ROW_TILE, t)
    tn = _col_tile(n_pad)
    return pl.pallas_call(
        _inproj_kernel,
        grid=(bsz, t // tm, n_pad // tn),
        in_specs=[
            pl.BlockSpec((1, tm, d), lambda b, i, j: (b, i, 0)),
            pl.BlockSpec((1, d), lambda b, i, j: (0, 0)),
            pl.BlockSpec((1, 1, d), lambda b, i, j: (b, 0, 0)),
            pl.BlockSpec((1, 1, d), lambda b, i, j: (b, 0, 0)),
            pl.BlockSpec((d, tn), lambda b, i, j: (0, j)),
        ],
        out_specs=pl.BlockSpec((1, tm, tn), lambda b, i, j: (b, i, j)),
        out_shape=jax.ShapeDtypeStruct((bsz, t, n_pad), BF16),
        scratch_shapes=[pltpu.VMEM((tm, d), BF16)],
        compiler_params=pltpu.CompilerParams(
            dimension_semantics=("parallel", "parallel", "arbitrary"),
            vmem_limit_bytes=VMEM_LIMIT),
        name="norm_inproj",
    )(x, norm_w.reshape(1, d), scale.reshape(bsz, 1, d), shift.reshape(bsz, 1, d), w)


def _outproj_kernel(og_ref, w_ref, x_ref, g_ref, fw_ref, o_ref, *, final):
    y = jnp.dot(og_ref[0], w_ref[...], preferred_element_type=F32)
    xn = x_ref[0] + g_ref[0] * y
    if final:
        ms = jnp.mean(xn * xn, axis=-1, keepdims=True)
        xn = xn * lax.rsqrt(ms + NORM_EPS) * fw_ref[...]
    o_ref[0] = xn


def _out_projection(og, w_out, x, gate, final_w, final):
    bsz, t, d = x.shape
    kdim = og.shape[-1]
    tm = min(512, t)
    return pl.pallas_call(
        functools.partial(_outproj_kernel, final=final),
        grid=(bsz, t // tm),
        in_specs=[
            pl.BlockSpec((1, tm, kdim), lambda b, i: (b, i, 0)),
            pl.BlockSpec((kdim, d), lambda b, i: (0, 0)),
            pl.BlockSpec((1, tm, d), lambda b, i: (b, i, 0)),
            pl.BlockSpec((1, 1, d), lambda b, i: (b, 0, 0)),
            pl.BlockSpec((1, d), lambda b, i: (0, 0)),
        ],
        out_specs=pl.BlockSpec((1, tm, d), lambda b, i: (b, i, 0)),
        out_shape=jax.ShapeDtypeStruct((bsz, t, d), F32),
        compiler_params=pltpu.CompilerParams(
            dimension_semantics=("parallel", "parallel"), vmem_limit_bytes=VMEM_LIMIT),
        name="outproj_residual",
    )(og, w_out.astype(BF16), x, gate.reshape(bsz, 1, d), final_w.reshape(1, d))


def _fill_shift_buffer(buf, x_ref, halo_ref, t):
    halo = halo_ref[0, HALO_ROWS - SUBLANE:, :].astype(F32)
    buf[0:SUBLANE, :] = jnp.where(t > 0, halo, 0.0)
    buf[SUBLANE:, :] = x_ref[0].astype(F32)


def _halo_spec(width, col_fn, tb, nt):
    per = tb // HALO_ROWS
    return pl.BlockSpec(
        (1, HALO_ROWS, width),
        lambda b, h, t: (b, jnp.maximum(jnp.minimum(t, nt - 1) * per - 1, 0), col_fn(h)))


def _gdn_kernel(alog_ref, dtb_ref,
                q_ref, k_ref, v_ref, z_ref, ba_ref,
                qh_ref, kh_ref, vh_ref,
                cq_ref, ck_ref, cv_ref, nw_ref,
                o_ref,
                s_scr, bq, bk, bv, u_scr, wq_scr, attn_scr, kdec_scr, egl_scr, *, heads, dk):
    h = pl.program_id(1)
    t = pl.program_id(2)
    tb = q_ref.shape[1]
    dv = v_ref.shape[2]
    taps = cq_ref.shape[0]
    slot = lax.rem(t, 2)
    prev = 1 - slot
    chunks = [slice(c * CHUNK, (c + 1) * CHUNK) for c in range(tb // CHUNK)]

    @pl.when(t == 0)
    def _():
        s_scr[...] = jnp.zeros_like(s_scr)
        u_scr[...] = jnp.zeros_like(u_scr)
        wq_scr[...] = jnp.zeros_like(wq_scr)
        attn_scr[...] = jnp.zeros_like(attn_scr)
        kdec_scr[...] = jnp.zeros_like(kdec_scr)
        egl_scr[...] = jnp.zeros_like(egl_scr)

    def conv_silu(x_ref, halo_ref, cw_ref, buf):
        _fill_shift_buffer(buf, x_ref, halo_ref, t)
        base = SUBLANE - (taps - 1)
        acc = buf[base:base + tb, :] * cw_ref[0:1, :]
        for j in range(1, taps):
            acc = acc + buf[base + j:base + j + tb, :] * cw_ref[j:j + 1, :]
        return _silu(acc)

    def local_work():
        q = conv_silu(q_ref, qh_ref, cq_ref, bq)
        k = conv_silu(k_ref, kh_ref, ck_ref, bk)
        v = conv_silu(v_ref, vh_ref, cv_ref, bv)
        q = q * lax.rsqrt(jnp.sum(q * q, axis=-1, keepdims=True) + L2_EPS) * (dk ** -0.5)
        k = k * lax.rsqrt(jnp.sum(k * k, axis=-1, keepdims=True) + L2_EPS)
        ba = ba_ref[0].astype(F32)
        lane = lax.broadcasted_iota(jnp.int32, ba.shape, 1)
        bcol = jnp.sum(jnp.where(lane == h, ba, 0.0), axis=1, keepdims=True)
        acol = jnp.sum(jnp.where(lane == h + heads, ba, 0.0), axis=1, keepdims=True)
        beta = _sigmoid(bcol)
        a_scale = jnp.exp(jnp.full((1, 1), alog_ref[h], F32))
        g = -a_scale * _softplus(acol + dtb_ref[h])
        ri, ci = _iotas(CHUNK)
        incl = ri >= ci
        strict = ri > ci
        gc_col, decay = [], []
        for sl in chunks:
            gcol = g[sl]
            g_row = jnp.sum(jnp.where(ri == ci, gcol, 0.0), axis=0, keepdims=True)
            gc_col.append(jnp.sum(jnp.where(incl, g_row, 0.0), axis=1, keepdims=True))
            gc_row = jnp.sum(jnp.where(ri <= ci, gcol, 0.0), axis=0, keepdims=True)
            decay.append(jnp.where(incl, jnp.exp(gc_col[-1] - gc_row), 0.0))
        kb = [k[sl] * beta[sl] for sl in chunks]
        yield
        kq = [_mm_nt(jnp.concatenate([x, q[sl]], axis=0), k[sl]) for x, sl in zip(kb, chunks)]
        yield
        tinv = yield from _unit_lower_inverses(
            [jnp.where(strict, m[:CHUNK] * d, 0.0) for m, d in zip(kq, decay)], CHUNK)
        e_gc = [jnp.exp(x) for x in gc_col]
        uw = [_mm(ti, jnp.concatenate([v[sl] * beta[sl], x * e], axis=1))
              for ti, sl, x, e in zip(tinv, chunks, kb, e_gc)]
        yield
        for c, sl in enumerate(chunks):
            g_last = gc_col[c][CHUNK - 1:CHUNK, :]
            u_scr[slot, sl, :] = uw[c][:, :dv]
            wq_scr[slot, c] = jnp.concatenate([uw[c][:, dv:], q[sl] * e_gc[c]], axis=0).astype(BF16)
            attn_scr[slot, c] = (kq[c][CHUNK:] * decay[c]).astype(BF16)
            kdec_scr[slot, c] = (k[sl] * jnp.exp(g_last - gc_col[c])).astype(BF16)
            egl_scr[slot, c] = jnp.broadcast_to(jnp.exp(g_last), egl_scr.shape[2:])

    def state_chain():
        nw = nw_ref[...]
        s = s_scr[...]
        for c, sl in enumerate(chunks):
            ws = _mm(wq_scr[prev, c], s)
            yield
            v_new = u_scr[prev, sl, :] - ws[:CHUNK]
            o = ws[CHUNK:] + _mm(attn_scr[prev, c], v_new)
            s = s * egl_scr[prev, c][0:1, 0:1] + _mm_tn(kdec_scr[prev, c], v_new)
            ms = jnp.mean(o * o, axis=-1, keepdims=True)
            on = o * lax.rsqrt(ms + NORM_EPS) * nw
            o_ref[0, sl, :] = (on * _silu(z_ref[0, sl, :].astype(F32))).astype(o_ref.dtype)
            yield
        s_scr[...] = s

    _alternate(local_work(), state_chain())


def _gdn_mixer(p, conv_w, a_log, dt_bias, norm_w, heads, dk, dv):
    bsz, t, _ = p.shape
    tb = min(TIME_BLOCK, t)
    nt = t // tb
    nc = tb // CHUNK
    qk = heads * dk
    kq = qk // dk
    kv = 2 * qk // dv
    kz = (2 * qk + heads * dv) // dv
    kba = (2 * qk + 2 * heads * dv) // LANE
    taps = conv_w.shape[0]
    smem = pl.BlockSpec(memory_space=pltpu.SMEM)
    cur = lambda i: jnp.minimum(i, nt - 1)
    prv = lambda i: jnp.maximum(i - 1, 0)
    in_specs = [
        smem, smem,
        pl.BlockSpec((1, tb, dk), lambda b, h, i: (b, cur(i), h)),
        pl.BlockSpec((1, tb, dk), lambda b, h, i: (b, cur(i), kq + h)),
        pl.BlockSpec((1, tb, dv), lambda b, h, i: (b, cur(i), kv + h)),
        pl.BlockSpec((1, tb, dv), lambda b, h, i: (b, prv(i), kz + h)),
        pl.BlockSpec((1, tb, LANE), lambda b, h, i: (b, cur(i), kba)),
        _halo_spec(dk, lambda h: h, tb, nt),
        _halo_spec(dk, lambda h: kq + h, tb, nt),
        _halo_spec(dv, lambda h: kv + h, tb, nt),
        pl.BlockSpec((taps, dk), lambda b, h, i: (0, h)),
        pl.BlockSpec((taps, dk), lambda b, h, i: (0, kq + h)),
        pl.BlockSpec((taps, dv), lambda b, h, i: (0, kv + h)),
        pl.BlockSpec((1, dv), lambda b, h, i: (0, 0)),
    ]
    return pl.pallas_call(
        functools.partial(_gdn_kernel, heads=heads, dk=dk),
        grid=(bsz, heads, nt + 1),
        in_specs=in_specs,
        out_specs=pl.BlockSpec((1, tb, dv), lambda b, h, i: (b, prv(i), h)),
        out_shape=jax.ShapeDtypeStruct((bsz, t, heads * dv), BF16),
        scratch_shapes=[
            pltpu.VMEM((dk, dv), F32),
            pltpu.VMEM((tb + SUBLANE, dk), F32),
            pltpu.VMEM((tb + SUBLANE, dk), F32),
            pltpu.VMEM((tb + SUBLANE, dv), F32),
            pltpu.VMEM((2, tb, dv), F32),
            pltpu.VMEM((2, nc, 2 * CHUNK, dk), BF16),
            pltpu.VMEM((2, nc, CHUNK, CHUNK), BF16),
            pltpu.VMEM((2, nc, CHUNK, dk), BF16),
            pltpu.VMEM((2, nc, SUBLANE, LANE), F32),
        ],
        compiler_params=pltpu.CompilerParams(
            dimension_semantics=("parallel", "parallel", "arbitrary"),
            vmem_limit_bytes=VMEM_LIMIT),
        name="gdn_mixer",
    )(a_log.astype(F32), dt_bias.astype(F32), p, p, p, p, p, p, p, p,
      conv_w, conv_w, conv_w, norm_w.reshape(1, dv))


def _hgrn_kernel(q_ref, f_ref, i_ref, z_ref, lb_ref, nw_ref, o_ref, s_scr):
    t = pl.program_id(2)
    tb = q_ref.shape[1]
    dk = q_ref.shape[2]

    @pl.when(t == 0)
    def _():
        s_scr[...] = jnp.zeros_like(s_scr)

    lb = lb_ref[...]
    q = _silu(q_ref[0].astype(F32))
    f = f_ref[0].astype(F32)
    a1 = jnp.log(lb)
    a2 = jnp.log1p(-lb) - _softplus(-f)
    log_f = jnp.maximum(a1, a2) + jnp.log1p(jnp.exp(-jnp.abs(a1 - a2)))
    k = (1.0 - lb) * _sigmoid(-f)
    v = i_ref[0].astype(F32)
    z = z_ref[0].astype(F32)
    nw = nw_ref[...]

    ri, ci = _iotas(CHUNK)
    rows = lax.broadcasted_iota(jnp.int32, (CHUNK, 1), 0)
    gcum_all = _chunk_cumsum(log_f)
    chunks = [slice(c * CHUNK, (c + 1) * CHUNK) for c in range(tb // CHUNK)]
    o_intra, s_add, s_mul, q_dec = [], [], [], []
    for sl in chunks:
        qc, kc, vc = q[sl], k[sl], v[sl]
        gcum = gcum_all[sl]
        g_last = gcum[CHUNK - 1:CHUNK, :]
        blocks = []
        for i in range(CHUNK // SUB):
            lo, hi = i * SUB, (i + 1) * SUB
            ref = gcum[lo - 1:lo, :] if i > 0 else jnp.zeros_like(g_last)
            qs = qc[lo:hi] * jnp.exp(gcum[lo:hi] - ref)
            ks = jnp.where(rows < hi, kc * jnp.exp(ref - gcum), 0.0)
            blocks.append(_mm_nt(qs, ks))
        attn = jnp.where(ri >= ci, jnp.concatenate(blocks, axis=0), 0.0)
        o_intra.append(_mm(attn, vc))
        s_add.append(_mm_tn(kc * jnp.exp(g_last - gcum), vc))
        s_mul.append(_row_to_col(jnp.exp(g_last), dk))
        q_dec.append(qc * jnp.exp(gcum))

    s = s_scr[...]
    for c, sl in enumerate(chunks):
        o = _mm(q_dec[c], s) + o_intra[c]
        s = s * s_mul[c] + s_add[c]
        ms = jnp.mean(o * o, axis=-1, keepdims=True)
        on = o * lax.rsqrt(ms + NORM_EPS) * nw
        o_ref[0, sl, :] = (on * _silu(z[sl])).astype(o_ref.dtype)
    s_scr[...] = s


def _hgrn_mixer(p, lower_bound, norm_w, heads, dk, dv):
    bsz, t, _ = p.shape
    tb = min(TIME_BLOCK, t)
    fdim = heads * dk
    kf = fdim // dk
    ki = 2 * fdim // dv
    kz = (2 * fdim + heads * dv) // dv
    return pl.pallas_call(
        _hgrn_kernel,
        grid=(bsz, heads, t // tb),
        in_specs=[
            pl.BlockSpec((1, tb, dk), lambda b, h, i: (b, i, h)),
            pl.BlockSpec((1, tb, dk), lambda b, h, i: (b, i, kf + h)),
            pl.BlockSpec((1, tb, dv), lambda b, h, i: (b, i, ki + h)),
            pl.BlockSpec((1, tb, dv), lambda b, h, i: (b, i, kz + h)),
            pl.BlockSpec((1, dk), lambda b, h, i: (0, h)),
            pl.BlockSpec((1, dv), lambda b, h, i: (0, 0)),
        ],
        out_specs=pl.BlockSpec((1, tb, dv), lambda b, h, i: (b, i, h)),
        out_shape=jax.ShapeDtypeStruct((bsz, t, heads * dv), BF16),
        scratch_shapes=[pltpu.VMEM((dk, dv), F32)],
        compiler_params=pltpu.CompilerParams(
            dimension_semantics=("parallel", "parallel", "arbitrary"),
            vmem_limit_bytes=VMEM_LIMIT),
        name="hgrn2_mixer",
    )(p, p, p, p, lower_bound.reshape(1, fdim), norm_w.reshape(1, dv))


def _rwkv_kernel(r_ref, k_ref, v_ref, z_ref, wa_ref,
                 rh_ref, kh_ref, vh_ref, zh_ref, wah_ref,
                 mu_r, mu_k, mu_v, mu_z, mu_wa,
                 w0_ref, a0_ref, kk_ref, ka_ref, rk_ref, lnw_ref, lnb_ref,
                 wl_ref, al_ref,
                 o_ref,
                 s_scr, buf, wr_scr, u0_scr, arb_scr, y0_scr, bdec_scr, sadd_scr, smul_scr,
                 bonus_scr, gate_scr, *, dh):
    t = pl.program_id(2)
    tb = r_ref.shape[1]
    lora = wl_ref.shape[0]
    heads = LANE // dh
    n = heads * CHUNK
    slot = lax.rem(t, 2)
    prev = 1 - slot
    chunks = [slice(c * CHUNK, (c + 1) * CHUNK) for c in range(tb // CHUNK)]

    @pl.when(t == 0)
    def _():
        for ref in (s_scr, wr_scr, u0_scr, arb_scr, y0_scr, bdec_scr, sadd_scr, smul_scr,
                    bonus_scr, gate_scr):
            ref[...] = jnp.zeros_like(ref)

    def shifted(x_ref, halo_ref, mu_ref):
        _fill_shift_buffer(buf, x_ref, halo_ref, t)
        cur = buf[SUBLANE:SUBLANE + tb, :]
        before = buf[SUBLANE - 1:SUBLANE - 1 + tb, :]
        return cur + (before - cur) * mu_ref[...]

    def local_work():
        r = shifted(r_ref, rh_ref, mu_r)
        k = shifted(k_ref, kh_ref, mu_k)
        v = shifted(v_ref, vh_ref, mu_v)
        z = shifted(z_ref, zh_ref, mu_z)
        wa = shifted(wa_ref, wah_ref, mu_wa)
        wd = wa[:, :lora]
        ad = wa[:, lora:]
        w_log = -_softplus(-(w0_ref[...] + _mm(jnp.tanh(wd), wl_ref[...]))) - 0.5
        log_w = -jnp.exp(w_log)
        a = _sigmoid(a0_ref[...] + _mm(ad, al_ref[...]))
        k2 = k * (1.0 + (a - 1.0) * ka_ref[...])
        kkx = k * kk_ref[...]
        kk = kkx * lax.rsqrt(_head_sum(kkx * kkx, dh) + RWKV_L2_EPS)
        bonus_scr[slot] = _head_sum(r * k2 * rk_ref[...], dh) * v
        gate_scr[slot] = _silu(z)
        kka = kk * a
        gcum = _chunk_cumsum(log_w)
        e_neg = jnp.exp(-gcum)
        a_t = -kk * jnp.exp(gcum - log_w)
        r_t = r * jnp.exp(gcum)
        b_t = kka * e_neg
        k_t = k2 * e_neg
        ri, ci = _iotas(n)
        same_head = _same_block(ri, ci, CHUNK)
        strict = same_head & (ri > ci)
        incl = same_head & (ri >= ci)
        a_bd = [_head_rows(a_t[sl], dh) for sl in chunks]
        r_bd = [_head_rows(r_t[sl], dh) for sl in chunks]
        v_bd = [_head_rows(v[sl], dh) for sl in chunks]
        yield
        pm = [_mm_nt(jnp.concatenate([a_bd[c], r_bd[c]], axis=0),
                     jnp.concatenate([b_t[sl]] * heads + [k_t[sl]] * heads, axis=0))
              for c, sl in enumerate(chunks)]
        yield
        a_krk = [jnp.concatenate([jnp.where(strict, m[:n, n:], 0.0),
                                  jnp.where(incl, m[n:, n:], 0.0)], axis=0) for m in pm]
        av = [_mm(x, vb) for x, vb in zip(a_krk, v_bd)]
        tinv = yield from _unit_lower_inverses(
            [jnp.where(strict, -m[:n, :n], 0.0) for m in pm], CHUNK)
        tw = [_mm(ti, jnp.concatenate([ab, x[:n]], axis=1)) for ti, ab, x in zip(tinv, a_bd, av)]
        yield
        for c, sl in enumerate(chunks):
            g_last = gcum[sl.stop - 1:sl.stop, :]
            e_last = jnp.exp(g_last - gcum[sl])
            wr_scr[slot, c] = jnp.concatenate([tw[c][:, :LANE], r_bd[c]], axis=0).astype(BF16)
            u0_scr[slot, c] = tw[c][:, LANE:]
            arb_scr[slot, c] = jnp.where(incl, pm[c][n:, :n], 0.0).astype(BF16)
            y0_scr[slot, c] = av[c][n:]
            bdec_scr[slot, c] = _head_rows(kka[sl] * e_last, dh).astype(BF16)
            sadd_scr[slot, c] = _mm_tn(_head_rows(k2[sl] * e_last, dh), v_bd[c])
            smul_scr[slot, c] = jnp.broadcast_to(_row_to_col(jnp.exp(g_last), LANE), (LANE, LANE))
        yield

    def state_chain():
        s = s_scr[...]
        for c, sl in enumerate(chunks):
            rs = _mm(wr_scr[prev, c], s)
            yield
            u = rs[:n] + u0_scr[prev, c]
            y_bd = rs[n:] + _mm(arb_scr[prev, c], u) + y0_scr[prev, c]
            s = s * smul_scr[prev, c] + _mm_tn(bdec_scr[prev, c], u) + sadd_scr[prev, c]
            y = y_bd[:CHUNK]
            for hh in range(1, heads):
                y = y + y_bd[hh * CHUNK:(hh + 1) * CHUNK]
            mean = _head_sum(y, dh) * (1.0 / dh)
            yc = y - mean
            var = _head_sum(yc * yc, dh) * (1.0 / dh)
            yn = yc * lax.rsqrt(var + dh * RWKV_GN_EPS_PER_DIM)
            out = (yn * lnw_ref[...] + lnb_ref[...] + bonus_scr[prev, sl, :]) * gate_scr[prev, sl, :]
            o_ref[0, sl, :] = out.astype(o_ref.dtype)
            yield
        s_scr[...] = s

    _alternate(local_work(), state_chain())


def _rwkv_mixer(p, mu, w0, w_lora_up, a0, a_lora_up, k_k, k_a, r_k, ln_w, ln_b, dh):
    bsz, t, _ = p.shape
    branch = w0.shape[0]
    lora = w_lora_up.shape[0]
    assert 2 * lora == LANE and LANE % dh == 0
    tb = min(TIME_BLOCK, t)
    nt = t // tb
    nc = tb // CHUNK
    n = (LANE // dh) * CHUNK
    nb = branch // LANE
    cur = lambda i: jnp.minimum(i, nt - 1)
    prv = lambda i: jnp.maximum(i - 1, 0)

    def stream(j):
        return pl.BlockSpec((1, tb, LANE), lambda b, h, i: (b, cur(i), j * nb + h))

    def stream_halo(j):
        return _halo_spec(LANE, lambda h: j * nb + h, tb, nt)

    def mu_spec(j):
        return pl.BlockSpec((1, LANE), lambda b, h, i: (0, j * nb + h))

    vec = pl.BlockSpec((1, LANE), lambda b, h, i: (0, h))
    wa_blk = 4 * nb
    in_specs = (
        [stream(j) for j in range(4)]
        + [pl.BlockSpec((1, tb, LANE), lambda b, h, i: (b, cur(i), wa_blk))]
        + [stream_halo(j) for j in range(4)]
        + [_halo_spec(LANE, lambda h: wa_blk, tb, nt)]
        + [mu_spec(j) for j in range(4)]
        + [pl.BlockSpec((1, LANE), lambda b, h, i: (0, wa_blk))]
        + [vec] * 7
        + [pl.BlockSpec((lora, LANE), lambda b, h, i: (0, h))] * 2
    )
    row = lambda a: a.reshape(1, -1)
    return pl.pallas_call(
        functools.partial(_rwkv_kernel, dh=dh),
        grid=(bsz, nb, nt + 1),
        in_specs=in_specs,
        out_specs=pl.BlockSpec((1, tb, LANE), lambda b, h, i: (b, prv(i), h)),
        out_shape=jax.ShapeDtypeStruct((bsz, t, branch), BF16),
        scratch_shapes=[
            pltpu.VMEM((LANE, LANE), F32),
            pltpu.VMEM((tb + SUBLANE, LANE), F32),
            pltpu.VMEM((2, nc, 2 * n, LANE), BF16),
            pltpu.VMEM((2, nc, n, LANE), F32),
            pltpu.VMEM((2, nc, n, n), BF16),
            pltpu.VMEM((2, nc, n, LANE), F32),
            pltpu.VMEM((2, nc, n, LANE), BF16),
            pltpu.VMEM((2, nc, LANE, LANE), F32),
            pltpu.VMEM((2, nc, LANE, LANE), F32),
            pltpu.VMEM((2, tb, LANE), F32),
            pltpu.VMEM((2, tb, LANE), F32),
        ],
        compiler_params=pltpu.CompilerParams(
            dimension_semantics=("parallel", "parallel", "arbitrary"),
            vmem_limit_bytes=VMEM_LIMIT),
        name="rwkv7_mixer",
    )(p, p, p, p, p, p, p, p, p, p,
      row(mu), row(mu), row(mu), row(mu), row(mu),
      row(w0), row(a0), row(k_k), row(k_a), row(r_k), row(ln_w), row(ln_b),
      w_lora_up, a_lora_up)


def kernel(x, c, ada_w, ada_b, norm_w, gdn_w_in, gdn_conv_w, gdn_a_log, gdn_dt_bias, gdn_norm_w, gdn_w_out, hgrn_w_in, hgrn_lb_logits, hgrn_norm_w, hgrn_w_out, rwkv_w_in, rwkv_mu, rwkv_w0, rwkv_w_lora_up, rwkv_a0, rwkv_a_lora_up, rwkv_k_k, rwkv_k_a, rwkv_r_k, rwkv_ln_w, rwkv_ln_b, rwkv_w_out, final_norm_w):
    depth, d = norm_w.shape
    n_mixers = 3
    gdn_heads = gdn_a_log.shape[1]
    gdn_dv = gdn_norm_w.shape[1]
    gdn_dk = (gdn_conv_w.shape[2] - gdn_heads * gdn_dv) // (2 * gdn_heads)
    hgrn_dv = hgrn_norm_w.shape[1]
    hgrn_f = hgrn_lb_logits.shape[1]
    hgrn_heads = (hgrn_w_in.shape[2] - 2 * hgrn_f) // (2 * hgrn_dv)
    hgrn_dk = hgrn_f // hgrn_heads
    rwkv_dh = rwkv_r_k.shape[2]

    x = x.astype(F32)
    mod = _modulations(c, ada_w, ada_b)
    lower_bounds = _lower_bounds(hgrn_lb_logits)
    for i in range(depth):
        shift, scale, gate = mod[i, :, :d], mod[i, :, d:2 * d], mod[i, :, 2 * d:]
        j = i // n_mixers
        kind = i % n_mixers
        if kind == 0:
            p = _in_projection(x, norm_w[i], scale, shift, gdn_w_in[j])
            og = _gdn_mixer(p, gdn_conv_w[j], gdn_a_log[j], gdn_dt_bias[j], gdn_norm_w[j],
                            gdn_heads, gdn_dk, gdn_dv)
            w_out = gdn_w_out[j]
        elif kind == 1:
            p = _in_projection(x, norm_w[i], scale, shift, hgrn_w_in[j])
            og = _hgrn_mixer(p, lower_bounds[i], hgrn_norm_w[j], hgrn_heads, hgrn_dk, hgrn_dv)
            w_out = hgrn_w_out[j]
        else:
            p = _in_projection(x, norm_w[i], scale, shift, rwkv_w_in[j])
            og = _rwkv_mixer(p, rwkv_mu[j], rwkv_w0[j], rwkv_w_lora_up[j], rwkv_a0[j],
                             rwkv_a_lora_up[j], rwkv_k_k[j], rwkv_k_a[j], rwkv_r_k[j],
                             rwkv_ln_w[j], rwkv_ln_b[j], rwkv_dh)
            w_out = rwkv_w_out[j]
        x = _out_projection(og, w_out, x, gate, final_norm_w, final=(i == depth - 1))
    return x
```

```python
import functools

import jax
import jax.numpy as jnp
from jax import lax
from jax.experimental import pallas as pl
from jax.experimental.pallas import tpu as pltpu

F32 = jnp.float32
BF16 = jnp.bfloat16
HIGHEST = lax.Precision.HIGHEST

NORM_EPS = 1e-6
L2_EPS = 1e-6
RWKV_L2_EPS = 1e-12
RWKV_GN_EPS_PER_DIM = 1e-5

LANE = 128
SUBLANE = 8
HALO_ROWS = 16
MXU_COLS = 256
ROW_TILE = 1024
MAX_COL_TILE = 3072
CHUNK = 64
SUB = 8
TIME_BLOCK = 512
GROUPS = 2
VMEM_LIMIT = 48 * 1024 * 1024


def _dot(a, b):
    return jnp.dot(a, b, preferred_element_type=F32, precision=HIGHEST)


def _mm(a, b):
    return jnp.dot(a.astype(BF16), b.astype(BF16), preferred_element_type=F32)


def _mm_nt(a, b):
    return lax.dot_general(a.astype(BF16), b.astype(BF16), (((1,), (1,)), ((), ())),
                           preferred_element_type=F32)


def _mm_tn(a, b):
    return lax.dot_general(a.astype(BF16), b.astype(BF16), (((0,), (0,)), ((), ())),
                           preferred_element_type=F32)


def _sigmoid(x):
    return 1.0 / (1.0 + jnp.exp(-x))


def _silu(x):
    return x * _sigmoid(x)


def _softplus(x):
    return jnp.maximum(x, 0.0) + jnp.log(1.0 + jnp.exp(-jnp.abs(x)))


def _iotas(n):
    ri = lax.broadcasted_iota(jnp.int32, (n, n), 0)
    ci = lax.broadcasted_iota(jnp.int32, (n, n), 1)
    return ri, ci


def _same_block(ri, ci, b):
    shift = b.bit_length() - 1
    assert 1 << shift == b
    return (ri >> shift) == (ci >> shift)


def _unit_lower_inverses(mats, top):
    n = mats[0].shape[0]
    ri, ci = _iotas(n)
    eye = (ri == ci).astype(F32)
    same = _same_block(ri, ci, SUB)
    nb = [jnp.where(same, a, 0.0) for a in mats]
    nb2 = [_mm(x, x) for x in nb]
    yield
    nb4 = [_mm(x, x) for x in nb2]
    inv = [_mm(eye - x, eye + x2) for x, x2 in zip(nb, nb2)]
    yield
    inv = [_mm(x, eye + x4) for x, x4 in zip(inv, nb4)]
    yield
    b = SUB
    while b < top:
        coupled = _same_block(ri, ci, 2 * b) & jnp.logical_not(_same_block(ri, ci, b))
        off = [jnp.where(coupled, a, 0.0) for a in mats]
        tmp = [_mm(x, o) for x, o in zip(inv, off)]
        yield
        inv = [x - _mm(t, x) for x, t in zip(inv, tmp)]
        yield
        b *= 2
    return inv


def _alternate(*gens_and_delays):
    live = [[g, d] for g, d in gens_and_delays]
    while live:
        for entry in list(live):
            if entry[1] > 0:
                entry[1] -= 1
                continue
            try:
                next(entry[0])
            except StopIteration:
                live.remove(entry)


def _chunk_cumsum(x):
    rows, w = x.shape
    ri, ci = _iotas(rows)
    tril = jnp.where(_same_block(ri, ci, CHUNK) & (ri >= ci), 1.0, 0.0).astype(BF16)
    hi = x.astype(BF16)
    rest = x - hi.astype(F32)
    mid = rest.astype(BF16)
    lo = (rest - mid.astype(F32)).astype(BF16)
    parts = jnp.dot(tril, jnp.concatenate([hi, mid, lo], axis=1), preferred_element_type=F32)
    return parts[:, :w] + parts[:, w:2 * w] + parts[:, 2 * w:]


def _head_sum(x, dh):
    lane = lax.broadcasted_iota(jnp.int32, x.shape, 1)
    out = jnp.zeros_like(x)
    for g in range(x.shape[1] // dh):
        in_group = (lane >= g * dh) & (lane < (g + 1) * dh)
        total = jnp.sum(jnp.where(in_group, x, 0.0), axis=-1, keepdims=True)
        out = jnp.where(in_group, total, out)
    return out


def _head_rows(x, dh):
    lane = lax.broadcasted_iota(jnp.int32, x.shape, 1)
    parts = []
    for g in range(x.shape[1] // dh):
        in_group = (lane >= g * dh) & (lane < (g + 1) * dh)
        parts.append(jnp.where(in_group, x, 0.0))
    return jnp.concatenate(parts, axis=0)


def _row_to_col(row, n):
    ri, ci = _iotas(n)
    return jnp.sum(jnp.where(ri == ci, row, 0.0), axis=1, keepdims=True)


def _mod_kernel(c_ref, w_ref, b_ref, o_ref):
    c = c_ref[...]
    o_ref[0] = _dot(_silu(c), w_ref[0]) + b_ref[0]


def _modulations(c, ada_w, ada_b):
    depth, d, d3 = ada_w.shape
    bsz = c.shape[0]
    rows = -(-bsz // SUBLANE) * SUBLANE
    c_pad = jnp.zeros((rows, d), F32).at[:bsz].set(c.astype(F32))
    out = pl.pallas_call(
        _mod_kernel,
        grid=(depth, d3 // d),
        in_specs=[
            pl.BlockSpec((rows, d), lambda i, j: (0, 0)),
            pl.BlockSpec((1, d, d), lambda i, j: (i, 0, j)),
            pl.BlockSpec((1, 1, d), lambda i, j: (i, 0, j)),
        ],
        out_specs=pl.BlockSpec((1, rows, d), lambda i, j: (i, 0, j)),
        out_shape=jax.ShapeDtypeStruct((depth, rows, d3), F32),
        compiler_params=pltpu.CompilerParams(
            dimension_semantics=("parallel", "parallel"), vmem_limit_bytes=VMEM_LIMIT),
        name="adaln_mod",
    )(c_pad, ada_w, ada_b.reshape(depth, 1, d3))
    return out[:, :bsz]


def _lb_kernel(x_ref, o_ref):
    x = x_ref[...]
    depth = x.shape[0]
    m = jnp.max(x, axis=0, keepdims=True)
    e = jnp.exp(x - m)
    p = e / jnp.sum(e, axis=0, keepdims=True)
    acc = jnp.zeros_like(p[0:1])
    for i in range(depth):
        acc = acc + p[i:i + 1]
        o_ref[i:i + 1, :] = acc - p[0:1]


def _lower_bounds(logits):
    return pl.pallas_call(
        _lb_kernel,
        out_shape=jax.ShapeDtypeStruct(logits.shape, F32),
        name="hgrn_lower_bounds",
    )(logits.astype(F32))


def _inproj_kernel(x_ref, nw_ref, sc_ref, sh_ref, w_ref, o_ref, h_scr):
    @pl.when(pl.program_id(2) == 0)
    def _():
        x = x_ref[0]
        ms = jnp.mean(x * x, axis=-1, keepdims=True)
        h = x * lax.rsqrt(ms + NORM_EPS) * nw_ref[...]
        h = h * (1.0 + sc_ref[0]) + sh_ref[0]
        h_scr[...] = h.astype(BF16)

    o_ref[0] = jnp.dot(h_scr[...], w_ref[...], preferred_element_type=F32).astype(o_ref.dtype)


def _col_tile(n):
    best = MXU_COLS
    for k in range(1, n // MXU_COLS + 1):
        if n % (k * MXU_COLS) == 0 and k * MXU_COLS <= MAX_COL_TILE:
            best = k * MXU_COLS
    return best


def _in_projection(x, norm_w, scale, shift, w_in):
    bsz, t, d = x.shape
    n = w_in.shape[1]
    n_pad = -(-n // MXU_COLS) * MXU_COLS
    w = w_in.astype(BF16)
    if n_pad != n:
        w = jnp.pad(w, ((0, 0), (0, n_pad - n)))
    tm = min(ROW_TILE, t)
    tn = _col_tile(n_pad)
    return pl.pallas_call(
        _inproj_kernel,
        grid=(bsz, t // tm, n_pad // tn),
        in_specs=[
            pl.BlockSpec((1, tm, d), lambda b, i, j: (b, i, 0)),
            pl.BlockSpec((1, d), lambda b, i, j: (0, 0)),
            pl.BlockSpec((1, 1, d), lambda b, i, j: (b, 0, 0)),
            pl.BlockSpec((1, 1, d), lambda b, i, j: (b, 0, 0)),
            pl.BlockSpec((d, tn), lambda b, i, j: (0, j)),
        ],
        out_specs=pl.BlockSpec((1, tm, tn), lambda b, i, j: (b, i, j)),
        out_shape=jax.ShapeDtypeStruct((bsz, t, n_pad), BF16),
        scratch_shapes=[pltpu.VMEM((tm, d), BF16)],
        compiler_params=pltpu.CompilerParams(
            dimension_semantics=("parallel", "parallel", "arbitrary"),
            vmem_limit_bytes=VMEM_LIMIT),
        name="norm_inproj",
    )(x, norm_w.reshape(1, d), scale.reshape(bsz, 1, d), shift.reshape(bsz, 1, d), w)


def _outproj_kernel(og_ref, w_ref, x_ref, g_ref, fw_ref, o_ref, *, final):
    y = jnp.dot(og_ref[0], w_ref[...], preferred_element_type=F32)
    xn = x_ref[0] + g_ref[0] * y
    if final:
        ms = jnp.mean(xn * xn, axis=-1, keepdims=True)
        xn = xn * lax.rsqrt(ms + NORM_EPS) * fw_ref[...]
    o_ref[0] = xn


def _out_projection(og, w_out, x, gate, final_w, final):
    bsz, t, d = x.shape
    kdim = og.shape[-1]
    tm = min(512, t)
    return pl.pallas_call(
        functools.partial(_outproj_kernel, final=final),
        grid=(bsz, t // tm),
        in_specs=[
            pl.BlockSpec((1, tm, kdim), lambda b, i: (b, i, 0)),
            pl.BlockSpec((kdim, d), lambda b, i: (0, 0)),
            pl.BlockSpec((1, tm, d), lambda b, i: (b, i, 0)),
            pl.BlockSpec((1, 1, d), lambda b, i: (b, 0, 0)),
            pl.BlockSpec((1, d), lambda b, i: (0, 0)),
        ],
        out_specs=pl.BlockSpec((1, tm, d), lambda b, i: (b, i, 0)),
        out_shape=jax.ShapeDtypeStruct((bsz, t, d), F32),
        compiler_params=pltpu.CompilerParams(
            dimension_semantics=("parallel", "parallel"), vmem_limit_bytes=VMEM_LIMIT),
        name="outproj_residual",
    )(og, w_out.astype(BF16), x, gate.reshape(bsz, 1, d), final_w.reshape(1, d))


def _fill_shift_buffer(buf, x_ref, halo_ref, t):
    halo = halo_ref[0, HALO_ROWS - SUBLANE:, :].astype(F32)
    buf[0:SUBLANE, :] = jnp.where(t > 0, halo, 0.0)
    buf[SUBLANE:, :] = x_ref[0].astype(F32)


def _halo_spec(width, col_fn, tb, nt):
    per = tb // HALO_ROWS
    return pl.BlockSpec(
        (1, HALO_ROWS, width),
        lambda b, h, t: (b, jnp.maximum(jnp.minimum(t, nt - 1) * per - 1, 0), col_fn(h)))


def _gdn_kernel(alog_ref, dtb_ref,
                q_ref, k_ref, v_ref, z_ref, ba_ref,
                qh_ref, kh_ref, vh_ref,
                cq_ref, ck_ref, cv_ref, nw_ref,
                o_ref,
                s_scr, bq, bk, bv, u_scr, wq_scr, attn_scr, kdec_scr, egl_scr, *, heads, dk):
    h = pl.program_id(1)
    t = pl.program_id(2)
    tb = q_ref.shape[1]
    dv = v_ref.shape[2]
    taps = cq_ref.shape[0]
    slot = lax.rem(t, 2)
    prev = 1 - slot
    rows = tb // GROUPS

    @pl.when(t == 0)
    def _():
        for ref in (s_scr, u_scr, wq_scr, attn_scr, kdec_scr, egl_scr):
            ref[...] = jnp.zeros_like(ref)

    _fill_shift_buffer(bq, q_ref, qh_ref, t)
    _fill_shift_buffer(bk, k_ref, kh_ref, t)
    _fill_shift_buffer(bv, v_ref, vh_ref, t)

    def conv_silu(buf, cw_ref, lo):
        base = SUBLANE - (taps - 1) + lo
        acc = buf[base:base + rows, :] * cw_ref[0:1, :]
        for j in range(1, taps):
            acc = acc + buf[base + j:base + j + rows, :] * cw_ref[j:j + 1, :]
        return _silu(acc)

    def local_work(group):
        lo = group * rows
        chunks = [slice(c * CHUNK, (c + 1) * CHUNK) for c in range(rows // CHUNK)]
        q = conv_silu(bq, cq_ref, lo)
        k = conv_silu(bk, ck_ref, lo)
        v = conv_silu(bv, cv_ref, lo)
        q = q * lax.rsqrt(jnp.sum(q * q, axis=-1, keepdims=True) + L2_EPS) * (dk ** -0.5)
        k = k * lax.rsqrt(jnp.sum(k * k, axis=-1, keepdims=True) + L2_EPS)
        ba = ba_ref[0, lo:lo + rows, :].astype(F32)
        lane = lax.broadcasted_iota(jnp.int32, ba.shape, 1)
        bcol = jnp.sum(jnp.where(lane == h, ba, 0.0), axis=1, keepdims=True)
        acol = jnp.sum(jnp.where(lane == h + heads, ba, 0.0), axis=1, keepdims=True)
        beta = _sigmoid(bcol)
        a_scale = jnp.exp(jnp.full((1, 1), alog_ref[h], F32))
        g = -a_scale * _softplus(acol + dtb_ref[h])
        ri, ci = _iotas(CHUNK)
        incl = ri >= ci
        strict = ri > ci
        gc_col, decay = [], []
        for sl in chunks:
            gcol = g[sl]
            g_row = jnp.sum(jnp.where(ri == ci, gcol, 0.0), axis=0, keepdims=True)
            gc_col.append(jnp.sum(jnp.where(incl, g_row, 0.0), axis=1, keepdims=True))
            gc_row = jnp.sum(jnp.where(ri <= ci, gcol, 0.0), axis=0, keepdims=True)
            decay.append(jnp.where(incl, jnp.exp(gc_col[-1] - gc_row), 0.0))
        kb = [k[sl] * beta[sl] for sl in chunks]
        yield
        kq = [_mm_nt(jnp.concatenate([x, q[sl]], axis=0), k[sl]) for x, sl in zip(kb, chunks)]
        yield
        tinv = yield from _unit_lower_inverses(
            [jnp.where(strict, m[:CHUNK] * d, 0.0) for m, d in zip(kq, decay)], CHUNK)
        e_gc = [jnp.exp(x) for x in gc_col]
        uw = [_mm(ti, jnp.concatenate([v[sl] * beta[sl], x * e], axis=1))
              for ti, sl, x, e in zip(tinv, chunks, kb, e_gc)]
        yield
        for c, sl in enumerate(chunks):
            gc = group * len(chunks) + c
            g_last = gc_col[c][CHUNK - 1:CHUNK, :]
            u_scr[slot, gc] = uw[c][:, :dv]
            wq_scr[slot, gc] = jnp.concatenate([uw[c][:, dv:], q[sl] * e_gc[c]], axis=0).astype(BF16)
            attn_scr[slot, gc] = (kq[c][CHUNK:] * decay[c]).astype(BF16)
            kdec_scr[slot, gc] = (k[sl] * jnp.exp(g_last - gc_col[c])).astype(BF16)
            egl_scr[slot, gc] = jnp.broadcast_to(jnp.exp(g_last), egl_scr.shape[2:])

    def state_chain():
        nw = nw_ref[...]
        s = s_scr[...]
        for c in range(tb // CHUNK):
            sl = slice(c * CHUNK, (c + 1) * CHUNK)
            ws = _mm(wq_scr[prev, c], s)
            yield
            v_new = u_scr[prev, c] - ws[:CHUNK]
            o = ws[CHUNK:] + _mm(attn_scr[prev, c], v_new)
            s = s * egl_scr[prev, c][0:1, 0:1] + _mm_tn(kdec_scr[prev, c], v_new)
            ms = jnp.mean(o * o, axis=-1, keepdims=True)
            on = o * lax.rsqrt(ms + NORM_EPS) * nw
            o_ref[0, sl, :] = (on * _silu(z_ref[0, sl, :].astype(F32))).astype(o_ref.dtype)
            yield
        s_scr[...] = s

    _alternate((state_chain(), 0), *[(local_work(g), g) for g in range(GROUPS)])


def _gdn_mixer(p, conv_w, a_log, dt_bias, norm_w, heads, dk, dv):
    bsz, t, _ = p.shape
    tb = min(TIME_BLOCK, t)
    nt = t // tb
    nc = tb // CHUNK
    qk = heads * dk
    kq = qk // dk
    kv = 2 * qk // dv
    kz = (2 * qk + heads * dv) // dv
    kba = (2 * qk + 2 * heads * dv) // LANE
    taps = conv_w.shape[0]
    smem = pl.BlockSpec(memory_space=pltpu.SMEM)
    cur = lambda i: jnp.minimum(i, nt - 1)
    prv = lambda i: jnp.maximum(i - 1, 0)
    in_specs = [
        smem, smem,
        pl.BlockSpec((1, tb, dk), lambda b, h, i: (b, cur(i), h)),
        pl.BlockSpec((1, tb, dk), lambda b, h, i: (b, cur(i), kq + h)),
        pl.BlockSpec((1, tb, dv), lambda b, h, i: (b, cur(i), kv + h)),
        pl.BlockSpec((1, tb, dv), lambda b, h, i: (b, prv(i), kz + h)),
        pl.BlockSpec((1, tb, LANE), lambda b, h, i: (b, cur(i), kba)),
        _halo_spec(dk, lambda h: h, tb, nt),
        _halo_spec(dk, lambda h: kq + h, tb, nt),
        _halo_spec(dv, lambda h: kv + h, tb, nt),
        pl.BlockSpec((taps, dk), lambda b, h, i: (0, h)),
        pl.BlockSpec((taps, dk), lambda b, h, i: (0, kq + h)),
        pl.BlockSpec((taps, dv), lambda b, h, i: (0, kv + h)),
        pl.BlockSpec((1, dv), lambda b, h, i: (0, 0)),
    ]
    return pl.pallas_call(
        functools.partial(_gdn_kernel, heads=heads, dk=dk),
        grid=(bsz, heads, nt + 1),
        in_specs=in_specs,
        out_specs=pl.BlockSpec((1, tb, dv), lambda b, h, i: (b, prv(i), h)),
        out_shape=jax.ShapeDtypeStruct((bsz, t, heads * dv), BF16),
        scratch_shapes=[
            pltpu.VMEM((dk, dv), F32),
            pltpu.VMEM((tb + SUBLANE, dk), F32),
            pltpu.VMEM((tb + SUBLANE, dk), F32),
            pltpu.VMEM((tb + SUBLANE, dv), F32),
            pltpu.VMEM((2, nc, CHUNK, dv), F32),
            pltpu.VMEM((2, nc, 2 * CHUNK, dk), BF16),
            pltpu.VMEM((2, nc, CHUNK, CHUNK), BF16),
            pltpu.VMEM((2, nc, CHUNK, dk), BF16),
            pltpu.VMEM((2, nc, SUBLANE, LANE), F32),
        ],
        compiler_params=pltpu.CompilerParams(
            dimension_semantics=("parallel", "parallel", "arbitrary"),
            vmem_limit_bytes=VMEM_LIMIT),
        name="gdn_mixer",
    )(a_log.astype(F32), dt_bias.astype(F32), p, p, p, p, p, p, p, p,
      conv_w, conv_w, conv_w, norm_w.reshape(1, dv))


def _hgrn_kernel(q_ref, f_ref, i_ref, z_ref, lb_ref, nw_ref, o_ref, s_scr):
    t = pl.program_id(2)
    tb = q_ref.shape[1]
    dk = q_ref.shape[2]

    @pl.when(t == 0)
    def _():
        s_scr[...] = jnp.zeros_like(s_scr)

    lb = lb_ref[...]
    q = _silu(q_ref[0].astype(F32))
    f = f_ref[0].astype(F32)
    a1 = jnp.log(lb)
    a2 = jnp.log(1.0 - lb) - _softplus(-f)
    log_f = jnp.maximum(a1, a2) + jnp.log(1.0 + jnp.exp(-jnp.abs(a1 - a2)))
    k = (1.0 - lb) * _sigmoid(-f)
    v = i_ref[0].astype(F32)
    z = z_ref[0].astype(F32)
    nw = nw_ref[...]

    ri, ci = _iotas(CHUNK)
    rows = lax.broadcasted_iota(jnp.int32, (CHUNK, 1), 0)
    gcum_all = _chunk_cumsum(log_f)
    chunks = [slice(c * CHUNK, (c + 1) * CHUNK) for c in range(tb // CHUNK)]
    o_intra, s_add, s_mul, q_dec = [], [], [], []
    for sl in chunks:
        qc, kc, vc = q[sl], k[sl], v[sl]
        gcum = gcum_all[sl]
        g_last = gcum[CHUNK - 1:CHUNK, :]
        blocks = []
        for i in range(CHUNK // SUB):
            lo, hi = i * SUB, (i + 1) * SUB
            ref = gcum[lo - 1:lo, :] if i > 0 else jnp.zeros_like(g_last)
            qs = qc[lo:hi] * jnp.exp(gcum[lo:hi] - ref)
            ks = jnp.where(rows < hi, kc * jnp.exp(ref - gcum), 0.0)
            blocks.append(_mm_nt(qs, ks))
        attn = jnp.where(ri >= ci, jnp.concatenate(blocks, axis=0), 0.0)
        o_intra.append(_mm(attn, vc))
        s_add.append(_mm_tn(kc * jnp.exp(g_last - gcum), vc))
        s_mul.append(_row_to_col(jnp.exp(g_last), dk))
        q_dec.append(qc * jnp.exp(gcum))

    s = s_scr[...]
    for c, sl in enumerate(chunks):
        o = _mm(q_dec[c], s) + o_intra[c]
        s = s * s_mul[c] + s_add[c]
        ms = jnp.mean(o * o, axis=-1, keepdims=True)
        on = o * lax.rsqrt(ms + NORM_EPS) * nw
        o_ref[0, sl, :] = (on * _silu(z[sl])).astype(o_ref.dtype)
    s_scr[...] = s


def _hgrn_mixer(p, lower_bound, norm_w, heads, dk, dv):
    bsz, t, _ = p.shape
    tb = min(TIME_BLOCK, t)
    fdim = heads * dk
    kf = fdim // dk
    ki = 2 * fdim // dv
    kz = (2 * fdim + heads * dv) // dv
    return pl.pallas_call(
        _hgrn_kernel,
        grid=(bsz, heads, t // tb),
        in_specs=[
            pl.BlockSpec((1, tb, dk), lambda b, h, i: (b, i, h)),
            pl.BlockSpec((1, tb, dk), lambda b, h, i: (b, i, kf + h)),
            pl.BlockSpec((1, tb, dv), lambda b, h, i: (b, i, ki + h)),
            pl.BlockSpec((1, tb, dv), lambda b, h, i: (b, i, kz + h)),
            pl.BlockSpec((1, dk), lambda b, h, i: (0, h)),
            pl.BlockSpec((1, dv), lambda b, h, i: (0, 0)),
        ],
        out_specs=pl.BlockSpec((1, tb, dv), lambda b, h, i: (b, i, h)),
        out_shape=jax.ShapeDtypeStruct((bsz, t, heads * dv), BF16),
        scratch_shapes=[pltpu.VMEM((dk, dv), F32)],
        compiler_params=pltpu.CompilerParams(
            dimension_semantics=("parallel", "parallel", "arbitrary"),
            vmem_limit_bytes=VMEM_LIMIT),
        name="hgrn2_mixer",
    )(p, p, p, p, lower_bound.reshape(1, fdim), norm_w.reshape(1, dv))


def _rwkv_kernel(r_ref, k_ref, v_ref, z_ref, wa_ref,
                 rh_ref, kh_ref, vh_ref, zh_ref, wah_ref,
                 mu_r, mu_k, mu_v, mu_z, mu_wa,
                 w0_ref, a0_ref, kk_ref, ka_ref, rk_ref, lnw_ref, lnb_ref,
                 wl_ref, al_ref,
                 o_ref,
                 s_scr, buf_r, buf_k, buf_v, buf_z, buf_wa,
                 wr_scr, u0_scr, arb_scr, y0_scr, bdec_scr, sadd_scr, smul_scr,
                 bonus_scr, gate_scr, *, dh):
    t = pl.program_id(2)
    tb = r_ref.shape[1]
    lora = wl_ref.shape[0]
    heads = LANE // dh
    n = heads * CHUNK
    slot = lax.rem(t, 2)
    prev = 1 - slot
    rows = tb // GROUPS

    @pl.when(t == 0)
    def _():
        for ref in (s_scr, wr_scr, u0_scr, arb_scr, y0_scr, bdec_scr, sadd_scr, smul_scr,
                    bonus_scr, gate_scr):
            ref[...] = jnp.zeros_like(ref)

    for x_ref, halo_ref, b in ((r_ref, rh_ref, buf_r), (k_ref, kh_ref, buf_k),
                               (v_ref, vh_ref, buf_v), (z_ref, zh_ref, buf_z),
                               (wa_ref, wah_ref, buf_wa)):
        _fill_shift_buffer(b, x_ref, halo_ref, t)

    def shifted(b, mu_ref, lo):
        cur = b[SUBLANE + lo:SUBLANE + lo + rows, :]
        before = b[SUBLANE - 1 + lo:SUBLANE - 1 + lo + rows, :]
        return cur + (before - cur) * mu_ref[...]

    def local_work(group):
        lo = group * rows
        chunks = [slice(c * CHUNK, (c + 1) * CHUNK) for c in range(rows // CHUNK)]
        r = shifted(buf_r, mu_r, lo)
        k = shifted(buf_k, mu_k, lo)
        v = shifted(buf_v, mu_v, lo)
        z = shifted(buf_z, mu_z, lo)
        wa = shifted(buf_wa, mu_wa, lo)
        wd = wa[:, :lora]
        ad = wa[:, lora:]
        w_log = -_softplus(-(w0_ref[...] + _mm(jnp.tanh(wd), wl_ref[...]))) - 0.5
        log_w = -jnp.exp(w_log)
        a = _sigmoid(a0_ref[...] + _mm(ad, al_ref[...]))
        k2 = k * (1.0 + (a - 1.0) * ka_ref[...])
        kkx = k * kk_ref[...]
        kk = kkx * lax.rsqrt(_head_sum(kkx * kkx, dh) + RWKV_L2_EPS)
        bonus_scr[slot, lo:lo + rows, :] = _head_sum(r * k2 * rk_ref[...], dh) * v
        gate_scr[slot, lo:lo + rows, :] = _silu(z)
        kka = kk * a
        gcum = _chunk_cumsum(log_w)
        e_neg = jnp.exp(-gcum)
        a_t = -kk * jnp.exp(gcum - log_w)
        r_t = r * jnp.exp(gcum)
        b_t = kka * e_neg
        k_t = k2 * e_neg
        ri, ci = _iotas(n)
        same_head = _same_block(ri, ci, CHUNK)
        strict = same_head & (ri > ci)
        incl = same_head & (ri >= ci)
        a_bd = [_head_rows(a_t[sl], dh) for sl in chunks]
        r_bd = [_head_rows(r_t[sl], dh) for sl in chunks]
        v_bd = [_head_rows(v[sl], dh) for sl in chunks]
        yield
        pm = [_mm_nt(jnp.concatenate([a_bd[c], r_bd[c]], axis=0),
                     jnp.concatenate([b_t[sl]] * heads + [k_t[sl]] * heads, axis=0))
              for c, sl in enumerate(chunks)]
        yield
        a_krk = [jnp.concatenate([jnp.where(strict, m[:n, n:], 0.0),
                                  jnp.where(incl, m[n:, n:], 0.0)], axis=0) for m in pm]
        av = [_mm(x, vb) for x, vb in zip(a_krk, v_bd)]
        tinv = yield from _unit_lower_inverses(
            [jnp.where(strict, -m[:n, :n], 0.0) for m in pm], CHUNK)
        tw = [_mm(ti, jnp.concatenate([ab, x[:n]], axis=1)) for ti, ab, x in zip(tinv, a_bd, av)]
        yield
        for c, sl in enumerate(chunks):
            gc = group * len(chunks) + c
            g_last = gcum[sl.stop - 1:sl.stop, :]
            e_last = jnp.exp(g_last - gcum[sl])
            wr_scr[slot, gc] = jnp.concatenate([tw[c][:, :LANE], r_bd[c]], axis=0).astype(BF16)
            u0_scr[slot, gc] = tw[c][:, LANE:]
            arb_scr[slot, gc] = jnp.where(incl, pm[c][n:, :n], 0.0).astype(BF16)
            y0_scr[slot, gc] = av[c][n:]
            bdec_scr[slot, gc] = _head_rows(kka[sl] * e_last, dh).astype(BF16)
            sadd_scr[slot, gc] = _mm_tn(_head_rows(k2[sl] * e_last, dh), v_bd[c])
            smul_scr[slot, gc] = jnp.broadcast_to(_row_to_col(jnp.exp(g_last), LANE), (LANE, LANE))
        yield

    def state_chain():
        s = s_scr[...]
        for c in range(tb // CHUNK):
            sl = slice(c * CHUNK, (c + 1) * CHUNK)
            rs = _mm(wr_scr[prev, c], s)
            yield
            u = rs[:n] + u0_scr[prev, c]
            y_bd = rs[n:] + _mm(arb_scr[prev, c], u) + y0_scr[prev, c]
            s = s * smul_scr[prev, c] + _mm_tn(bdec_scr[prev, c], u) + sadd_scr[prev, c]
            y = y_bd[:CHUNK]
            for hh in range(1, heads):
                y = y + y_bd[hh * CHUNK:(hh + 1) * CHUNK]
            mean = _head_sum(y, dh) * (1.0 / dh)
            yc = y - mean
            var = _head_sum(yc * yc, dh) * (1.0 / dh)
            yn = yc * lax.rsqrt(var + dh * RWKV_GN_EPS_PER_DIM)
            out = (yn * lnw_ref[...] + lnb_ref[...] + bonus_scr[prev, sl, :]) * gate_scr[prev, sl, :]
            o_ref[0, sl, :] = out.astype(o_ref.dtype)
            yield
        s_scr[...] = s

    _alternate((state_chain(), 0), *[(local_work(g), g) for g in range(GROUPS)])


def _rwkv_mixer(p, mu, w0, w_lora_up, a0, a_lora_up, k_k, k_a, r_k, ln_w, ln_b, dh):
    bsz, t, _ = p.shape
    branch = w0.shape[0]
    lora = w_lora_up.shape[0]
    assert 2 * lora == LANE and LANE % dh == 0
    tb = min(TIME_BLOCK, t)
    nt = t // tb
    nc = tb // CHUNK
    n = (LANE // dh) * CHUNK
    nb = branch // LANE
    cur = lambda i: jnp.minimum(i, nt - 1)
    prv = lambda i: jnp.maximum(i - 1, 0)

    def stream(j):
        return pl.BlockSpec((1, tb, LANE), lambda b, h, i: (b, cur(i), j * nb + h))

    def stream_halo(j):
        return _halo_spec(LANE, lambda h: j * nb + h, tb, nt)

    def mu_spec(j):
        return pl.BlockSpec((1, LANE), lambda b, h, i: (0, j * nb + h))

    vec = pl.BlockSpec((1, LANE), lambda b, h, i: (0, h))
    wa_blk = 4 * nb
    in_specs = (
        [stream(j) for j in range(4)]
        + [pl.BlockSpec((1, tb, LANE), lambda b, h, i: (b, cur(i), wa_blk))]
        + [stream_halo(j) for j in range(4)]
        + [_halo_spec(LANE, lambda h: wa_blk, tb, nt)]
        + [mu_spec(j) for j in range(4)]
        + [pl.BlockSpec((1, LANE), lambda b, h, i: (0, wa_blk))]
        + [vec] * 7
        + [pl.BlockSpec((lora, LANE), lambda b, h, i: (0, h))] * 2
    )
    row = lambda a: a.reshape(1, -1)
    return pl.pallas_call(
        functools.partial(_rwkv_kernel, dh=dh),
        grid=(bsz, nb, nt + 1),
        in_specs=in_specs,
        out_specs=pl.BlockSpec((1, tb, LANE), lambda b, h, i: (b, prv(i), h)),
        out_shape=jax.ShapeDtypeStruct((bsz, t, branch), BF16),
        scratch_shapes=[
            pltpu.VMEM((LANE, LANE), F32),
        ] + [pltpu.VMEM((tb + SUBLANE, LANE), F32)] * 5 + [
            pltpu.VMEM((2, nc, 2 * n, LANE), BF16),
            pltpu.VMEM((2, nc, n, LANE), F32),
            pltpu.VMEM((2, nc, n, n), BF16),
            pltpu.VMEM((2, nc, n, LANE), F32),
            pltpu.VMEM((2, nc, n, LANE), BF16),
            pltpu.VMEM((2, nc, LANE, LANE), F32),
            pltpu.VMEM((2, nc, LANE, LANE), F32),
            pltpu.VMEM((2, tb, LANE), F32),
            pltpu.VMEM((2, tb, LANE), F32),
        ],
        compiler_params=pltpu.CompilerParams(
            dimension_semantics=("parallel", "parallel", "arbitrary"),
            vmem_limit_bytes=VMEM_LIMIT),
        name="rwkv7_mixer",
    )(p, p, p, p, p, p, p, p, p, p,
      row(mu), row(mu), row(mu), row(mu), row(mu),
      row(w0), row(a0), row(k_k), row(k_a), row(r_k), row(ln_w), row(ln_b),
      w_lora_up, a_lora_up)


def kernel(x, c, ada_w, ada_b, norm_w, gdn_w_in, gdn_conv_w, gdn_a_log, gdn_dt_bias, gdn_norm_w, gdn_w_out, hgrn_w_in, hgrn_lb_logits, hgrn_norm_w, hgrn_w_out, rwkv_w_in, rwkv_mu, rwkv_w0, rwkv_w_lora_up, rwkv_a0, rwkv_a_lora_up, rwkv_k_k, rwkv_k_a, rwkv_r_k, rwkv_ln_w, rwkv_ln_b, rwkv_w_out, final_norm_w):
    depth, d = norm_w.shape
    n_mixers = 3
    gdn_heads = gdn_a_log.shape[1]
    gdn_dv = gdn_norm_w.shape[1]
    gdn_dk = (gdn_conv_w.shape[2] - gdn_heads * gdn_dv) // (2 * gdn_heads)
    hgrn_dv = hgrn_norm_w.shape[1]
    hgrn_f = hgrn_lb_logits.shape[1]
    hgrn_heads = (hgrn_w_in.shape[2] - 2 * hgrn_f) // (2 * hgrn_dv)
    hgrn_dk = hgrn_f // hgrn_heads
    rwkv_dh = rwkv_r_k.shape[2]

    x = x.astype(F32)
    mod = _modulations(c, ada_w, ada_b)
    lower_bounds = _lower_bounds(hgrn_lb_logits)
    for i in range(depth):
        shift, scale, gate = mod[i, :, :d], mod[i, :, d:2 * d], mod[i, :, 2 * d:]
        j = i // n_mixers
        kind = i % n_mixers
        if kind == 0:
            p = _in_projection(x, norm_w[i], scale, shift, gdn_w_in[j])
            og = _gdn_mixer(p, gdn_conv_w[j], gdn_a_log[j], gdn_dt_bias[j], gdn_norm_w[j],
                            gdn_heads, gdn_dk, gdn_dv)
            w_out = gdn_w_out[j]
        elif kind == 1:
            p = _in_projection(x, norm_w[i], scale, shift, hgrn_w_in[j])
            og = _hgrn_mixer(p, lower_bounds[i], hgrn_norm_w[j], hgrn_heads, hgrn_dk, hgrn_dv)
            w_out = hgrn_w_out[j]
        else:
            p = _in_projection(x, norm_w[i], scale, shift, rwkv_w_in[j])
            og = _rwkv_mixer(p, rwkv_mu[j], rwkv_w0[j], rwkv_w_lora_up[j], rwkv_a0[j],
                             rwkv_a_lora_up[j], rwkv_k_k[j], rwkv_k_a[j], rwkv_r_k[j],
                             rwkv_ln_w[j], rwkv_ln_b[j], rwkv_dh)
            w_out = rwkv_w_out[j]
        x = _out_projection(og, w_out, x, gate, final_norm_w, final=(i == depth - 1))
    return x
```

```python
import functools

import jax
import jax.numpy as jnp
from jax import lax
from jax.experimental import pallas as pl
from jax.experimental.pallas import tpu as pltpu

F32 = jnp.float32
BF16 = jnp.bfloat16
HIGHEST = lax.Precision.HIGHEST

NORM_EPS = 1e-6
L2_EPS = 1e-6
RWKV_L2_EPS = 1e-12
RWKV_GN_EPS_PER_DIM = 1e-5

LANE = 128
SUBLANE = 8
HALO_ROWS = 16
MXU_COLS = 256
ROW_TILE = 1024
MAX_COL_TILE = 3072
CHUNK = 64
SUB = 8
TIME_BLOCK = 512
GROUPS = 2
VMEM_LIMIT = 48 * 1024 * 1024


def _dot(a, b):
    return jnp.dot(a, b, preferred_element_type=F32, precision=HIGHEST)


def _mm(a, b):
    return jnp.dot(a.astype(BF16), b.astype(BF16), preferred_element_type=F32)


def _mm_nt(a, b):
    return lax.dot_general(a.astype(BF16), b.astype(BF16), (((1,), (1,)), ((), ())),
                           preferred_element_type=F32)


def _mm_tn(a, b):
    return lax.dot_general(a.astype(BF16), b.astype(BF16), (((0,), (0,)), ((), ())),
                           preferred_element_type=F32)


def _sigmoid(x):
    return 1.0 / (1.0 + jnp.exp(-x))


def _silu(x):
    return x * _sigmoid(x)


def _softplus(x):
    return jnp.maximum(x, 0.0) + jnp.log(1.0 + jnp.exp(-jnp.abs(x)))


def _iotas(n):
    ri = lax.broadcasted_iota(jnp.int32, (n, n), 0)
    ci = lax.broadcasted_iota(jnp.int32, (n, n), 1)
    return ri, ci


def _same_block(ri, ci, b):
    shift = b.bit_length() - 1
    assert 1 << shift == b
    return (ri >> shift) == (ci >> shift)


def _unit_lower_inverses(mats, top):
    n = mats[0].shape[0]
    ri, ci = _iotas(n)
    eye = (ri == ci).astype(F32)
    same = _same_block(ri, ci, SUB)
    nb = [jnp.where(same, a, 0.0) for a in mats]
    nb2 = [_mm(x, x) for x in nb]
    yield
    nb4 = [_mm(x, x) for x in nb2]
    inv = [_mm(eye - x, eye + x2) for x, x2 in zip(nb, nb2)]
    yield
    inv = [_mm(x, eye + x4) for x, x4 in zip(inv, nb4)]
    yield
    b = SUB
    while b < top:
        coupled = _same_block(ri, ci, 2 * b) & jnp.logical_not(_same_block(ri, ci, b))
        off = [jnp.where(coupled, a, 0.0) for a in mats]
        tmp = [_mm(x, o) for x, o in zip(inv, off)]
        yield
        inv = [x - _mm(t, x) for x, t in zip(inv, tmp)]
        yield
        b *= 2
    return inv


def _alternate(*gens_and_delays):
    live = [[g, d] for g, d in gens_and_delays]
    while live:
        for entry in list(live):
            if entry[1] > 0:
                entry[1] -= 1
                continue
            try:
                next(entry[0])
            except StopIteration:
                live.remove(entry)


def _after(x, anchor):
    if anchor is None:
        return x
    return x + jnp.minimum(jnp.abs(anchor), 0.0)


def _chunk_cumsum(x):
    rows, w = x.shape
    ri, ci = _iotas(rows)
    tril = jnp.where(_same_block(ri, ci, CHUNK) & (ri >= ci), 1.0, 0.0).astype(BF16)
    hi = x.astype(BF16)
    rest = x - hi.astype(F32)
    mid = rest.astype(BF16)
    lo = (rest - mid.astype(F32)).astype(BF16)
    parts = jnp.dot(tril, jnp.concatenate([hi, mid, lo], axis=1), preferred_element_type=F32)
    return parts[:, :w] + parts[:, w:2 * w] + parts[:, 2 * w:]


def _head_sum(x, dh):
    lane = lax.broadcasted_iota(jnp.int32, x.shape, 1)
    out = jnp.zeros_like(x)
    for g in range(x.shape[1] // dh):
        in_group = (lane >= g * dh) & (lane < (g + 1) * dh)
        total = jnp.sum(jnp.where(in_group, x, 0.0), axis=-1, keepdims=True)
        out = jnp.where(in_group, total, out)
    return out


def _head_rows(x, dh):
    lane = lax.broadcasted_iota(jnp.int32, x.shape, 1)
    parts = []
    for g in range(x.shape[1] // dh):
        in_group = (lane >= g * dh) & (lane < (g + 1) * dh)
        parts.append(jnp.where(in_group, x, 0.0))
    return jnp.concatenate(parts, axis=0)


def _row_to_col(row, n):
    ri, ci = _iotas(n)
    return jnp.sum(jnp.where(ri == ci, row, 0.0), axis=1, keepdims=True)


def _mod_kernel(c_ref, w_ref, b_ref, o_ref):
    c = c_ref[...]
    o_ref[0] = _dot(_silu(c), w_ref[0]) + b_ref[0]


def _modulations(c, ada_w, ada_b):
    depth, d, d3 = ada_w.shape
    bsz = c.shape[0]
    rows = -(-bsz // SUBLANE) * SUBLANE
    c_pad = jnp.zeros((rows, d), F32).at[:bsz].set(c.astype(F32))
    out = pl.pallas_call(
        _mod_kernel,
        grid=(depth, d3 // d),
        in_specs=[
            pl.BlockSpec((rows, d), lambda i, j: (0, 0)),
            pl.BlockSpec((1, d, d), lambda i, j: (i, 0, j)),
            pl.BlockSpec((1, 1, d), lambda i, j: (i, 0, j)),
        ],
        out_specs=pl.BlockSpec((1, rows, d), lambda i, j: (i, 0, j)),
        out_shape=jax.ShapeDtypeStruct((depth, rows, d3), F32),
        compiler_params=pltpu.CompilerParams(
            dimension_semantics=("parallel", "parallel"), vmem_limit_bytes=VMEM_LIMIT),
        name="adaln_mod",
    )(c_pad, ada_w, ada_b.reshape(depth, 1, d3))
    return out[:, :bsz]


def _lb_kernel(x_ref, o_ref):
    x = x_ref[...]
    depth = x.shape[0]
    m = jnp.max(x, axis=0, keepdims=True)
    e = jnp.exp(x - m)
    p = e / jnp.sum(e, axis=0, keepdims=True)
    acc = jnp.zeros_like(p[0:1])
    for i in range(depth):
        acc = acc + p[i:i + 1]
        o_ref[i:i + 1, :] = acc - p[0:1]


def _lower_bounds(logits):
    return pl.pallas_call(
        _lb_kernel,
        out_shape=jax.ShapeDtypeStruct(logits.shape, F32),
        name="hgrn_lower_bounds",
    )(logits.astype(F32))


def _inproj_kernel(x_ref, nw_ref, sc_ref, sh_ref, w_ref, o_ref, h_scr):
    @pl.when(pl.program_id(2) == 0)
    def _():
        x = x_ref[0]
        ms = jnp.mean(x * x, axis=-1, keepdims=True)
        h = x * lax.rsqrt(ms + NORM_EPS) * nw_ref[...]
        h = h * (1.0 + sc_ref[0]) + sh_ref[0]
        h_scr[...] = h.astype(BF16)

    o_ref[0] = jnp.dot(h_scr[...], w_ref[...], preferred_element_type=F32).astype(o_ref.dtype)


def _col_tile(n):
    best = MXU_COLS
    for k in range(1, n // MXU_COLS + 1):
        if n % (k * MXU_COLS) == 0 and k * MXU_COLS <= MAX_COL_TILE:
            best = k * MXU_COLS
    return best


def _in_projection(x, norm_w, scale, shift, w_in):
    bsz, t, d = x.shape
    n = w_in.shape[1]
    n_pad = -(-n // MXU_COLS) * MXU_COLS
    w = w_in.astype(BF16)
    if n_pad != n:
        w = jnp.pad(w, ((0, 0), (0, n_pad - n)))
    tm = min(ROW_TILE, t)
    tn = _col_tile(n_pad)
    return pl.pallas_call(
        _inproj_kernel,
        grid=(bsz, t // tm, n_pad // tn),
        in_specs=[
            pl.BlockSpec((1, tm, d), lambda b, i, j: (b, i, 0)),
            pl.BlockSpec((1, d), lambda b, i, j: (0, 0)),
            pl.BlockSpec((1, 1, d), lambda b, i, j: (b, 0, 0)),
            pl.BlockSpec((1, 1, d), lambda b, i, j: (b, 0, 0)),
            pl.BlockSpec((d, tn), lambda b, i, j: (0, j)),
        ],
        out_specs=pl.BlockSpec((1, tm, tn), lambda b, i, j: (b, i, j)),
        out_shape=jax.ShapeDtypeStruct((bsz, t, n_pad), BF16),
        scratch_shapes=[pltpu.VMEM((tm, d), BF16)],
        compiler_params=pltpu.CompilerParams(
            dimension_semantics=("parallel", "parallel", "arbitrary"),
            vmem_limit_bytes=VMEM_LIMIT),
        name="norm_inproj",
    )(x, norm_w.reshape(1, d), scale.reshape(bsz, 1, d), shift.reshape(bsz, 1, d), w)


def _outproj_kernel(og_ref, w_ref, x_ref, g_ref, fw_ref, o_ref, *, final):
    y = jnp.dot(og_ref[0], w_ref[...], preferred_element_type=F32)
    xn = x_ref[0] + g_ref[0] * y
    if final:
        ms = jnp.mean(xn * xn, axis=-1, keepdims=True)
        xn = xn * lax.rsqrt(ms + NORM_EPS) * fw_ref[...]
    o_ref[0] = xn


def _out_projection(og, w_out, x, gate, final_w, final):
    bsz, t, d = x.shape
    kdim = og.shape[-1]
    tm = min(512, t)
    return pl.pallas_call(
        functools.partial(_outproj_kernel, final=final),
        grid=(bsz, t // tm),
        in_specs=[
            pl.BlockSpec((1, tm, kdim), lambda b, i: (b, i, 0)),
            pl.BlockSpec((kdim, d), lambda b, i: (0, 0)),
            pl.BlockSpec((1, tm, d), lambda b, i: (b, i, 0)),
            pl.BlockSpec((1, 1, d), lambda b, i: (b, 0, 0)),
            pl.BlockSpec((1, d), lambda b, i: (0, 0)),
        ],
        out_specs=pl.BlockSpec((1, tm, d), lambda b, i: (b, i, 0)),
        out_shape=jax.ShapeDtypeStruct((bsz, t, d), F32),
        compiler_params=pltpu.CompilerParams(
            dimension_semantics=("parallel", "parallel"), vmem_limit_bytes=VMEM_LIMIT),
        name="outproj_residual",
    )(og, w_out.astype(BF16), x, gate.reshape(bsz, 1, d), final_w.reshape(1, d))


def _fill_shift_buffer(buf, x_ref, halo_ref, has_prev):
    halo = halo_ref[0, HALO_ROWS - SUBLANE:, :].astype(F32)
    buf[0:SUBLANE, :] = jnp.where(has_prev, halo, 0.0)
    buf[SUBLANE:, :] = x_ref[0].astype(F32)


class _Pipeline:
    def __init__(self, bsz, heads, nt):
        self.bsz, self.heads, self.nt = bsz, heads, nt
        self.blocks = bsz * heads * nt
        self.steps = self.blocks + 1

    def decode(self, blk):
        return blk // (self.heads * self.nt), (blk // self.nt) % self.heads, blk % self.nt

    def local_block(self, l):
        return jnp.minimum(l, self.blocks - 1)

    def chain_block(self, l):
        return jnp.maximum(l - 1, 0)

    def stream(self, shape, col_fn):
        def index(l):
            b, h, i = self.decode(self.local_block(l))
            return b, i, col_fn(h)
        return pl.BlockSpec(shape, index)

    def halo(self, width, col_fn, tb):
        per = tb // HALO_ROWS

        def index(l):
            b, h, i = self.decode(self.local_block(l))
            return b, jnp.maximum(i * per - 1, 0), col_fn(h)
        return pl.BlockSpec((1, HALO_ROWS, width), index)

    def per_head(self, shape, col_fn):
        def index(l):
            _, h, _ = self.decode(self.local_block(l))
            return 0, col_fn(h)
        return pl.BlockSpec(shape, index)

    def chain(self, shape, col_fn):
        def index(l):
            b, h, i = self.decode(self.chain_block(l))
            return b, i, col_fn(h)
        return pl.BlockSpec(shape, index)

    def chain_per_head(self, shape, col_fn):
        def index(l):
            _, h, _ = self.decode(self.chain_block(l))
            return 0, col_fn(h)
        return pl.BlockSpec(shape, index)


def _gdn_kernel(alog_ref, dtb_ref,
                q_ref, k_ref, v_ref, z_ref, ba_ref,
                qh_ref, kh_ref, vh_ref,
                cq_ref, ck_ref, cv_ref, nw_ref,
                o_ref,
                s_scr, bq, bk, bv, u_scr, wq_scr, attn_scr, kdec_scr, egl_scr, *, pipe, dk):
    l = pl.program_id(0)
    _, h, i_local = pipe.decode(pipe.local_block(l))
    heads = pipe.heads
    tb = q_ref.shape[1]
    dv = v_ref.shape[2]
    taps = cq_ref.shape[0]
    slot = lax.rem(l, 2)
    prev = 1 - slot
    rows = tb // GROUPS

    @pl.when(l == 0)
    def _():
        for ref in (s_scr, u_scr, wq_scr, attn_scr, kdec_scr, egl_scr):
            ref[...] = jnp.zeros_like(ref)

    _fill_shift_buffer(bq, q_ref, qh_ref, i_local > 0)
    _fill_shift_buffer(bk, k_ref, kh_ref, i_local > 0)
    _fill_shift_buffer(bv, v_ref, vh_ref, i_local > 0)

    def conv_silu(buf, cw_ref, lo, after):
        cw = _after(cw_ref[...], after)
        base = SUBLANE - (taps - 1) + lo
        acc = buf[base:base + rows, :] * cw[0:1, :]
        for j in range(1, taps):
            acc = acc + buf[base + j:base + j + rows, :] * cw[j:j + 1, :]
        return _silu(acc)

    first_matmul = {}

    def local_work(group):
        lo = group * rows
        chunks = [slice(c * CHUNK, (c + 1) * CHUNK) for c in range(rows // CHUNK)]
        after = first_matmul.get(group - 1)
        q = conv_silu(bq, cq_ref, lo, after)
        k = conv_silu(bk, ck_ref, lo, after)
        v = conv_silu(bv, cv_ref, lo, after)
        q = q * lax.rsqrt(jnp.sum(q * q, axis=-1, keepdims=True) + L2_EPS) * (dk ** -0.5)
        k = k * lax.rsqrt(jnp.sum(k * k, axis=-1, keepdims=True) + L2_EPS)
        ba = ba_ref[0, lo:lo + rows, :].astype(F32)
        lane = lax.broadcasted_iota(jnp.int32, ba.shape, 1)
        bcol = jnp.sum(jnp.where(lane == h, ba, 0.0), axis=1, keepdims=True)
        acol = jnp.sum(jnp.where(lane == h + heads, ba, 0.0), axis=1, keepdims=True)
        beta = _sigmoid(bcol)
        a_scale = jnp.exp(jnp.full((1, 1), alog_ref[h], F32))
        g = -a_scale * _softplus(acol + dtb_ref[h])
        ri, ci = _iotas(CHUNK)
        incl = ri >= ci
        strict = ri > ci
        gc_col, decay = [], []
        for sl in chunks:
            gcol = g[sl]
            g_row = jnp.sum(jnp.where(ri == ci, gcol, 0.0), axis=0, keepdims=True)
            gc_col.append(jnp.sum(jnp.where(incl, g_row, 0.0), axis=1, keepdims=True))
            gc_row = jnp.sum(jnp.where(ri <= ci, gcol, 0.0), axis=0, keepdims=True)
            decay.append(jnp.where(incl, jnp.exp(gc_col[-1] - gc_row), 0.0))
        kb = [k[sl] * beta[sl] for sl in chunks]
        yield
        kq = [_mm_nt(jnp.concatenate([x, q[sl]], axis=0), k[sl]) for x, sl in zip(kb, chunks)]
        first_matmul[group] = kq[0][0:1, 0:1]
        yield
        tinv = yield from _unit_lower_inverses(
            [jnp.where(strict, m[:CHUNK] * d, 0.0) for m, d in zip(kq, decay)], CHUNK)
        e_gc = [jnp.exp(x) for x in gc_col]
        uw = [_mm(ti, jnp.concatenate([v[sl] * beta[sl], x * e], axis=1))
              for ti, sl, x, e in zip(tinv, chunks, kb, e_gc)]
        yield
        for c, sl in enumerate(chunks):
            gc = group * len(chunks) + c
            g_last = gc_col[c][CHUNK - 1:CHUNK, :]
            u_scr[slot, gc] = uw[c][:, :dv]
            wq_scr[slot, gc] = jnp.concatenate([uw[c][:, dv:], q[sl] * e_gc[c]], axis=0).astype(BF16)
            attn_scr[slot, gc] = (kq[c][CHUNK:] * decay[c]).astype(BF16)
            kdec_scr[slot, gc] = (k[sl] * jnp.exp(g_last - gc_col[c])).astype(BF16)
            egl_scr[slot, gc] = jnp.broadcast_to(jnp.exp(g_last), egl_scr.shape[2:])

    def state_chain():
        nw = nw_ref[...]
        s = jnp.where(lax.rem(l - 1, pipe.nt) == 0, 0.0, s_scr[...])
        for c in range(tb // CHUNK):
            sl = slice(c * CHUNK, (c + 1) * CHUNK)
            ws = _mm(wq_scr[prev, c], s)
            yield
            v_new = u_scr[prev, c] - ws[:CHUNK]
            o = ws[CHUNK:] + _mm(attn_scr[prev, c], v_new)
            s = s * egl_scr[prev, c][0:1, 0:1] + _mm_tn(kdec_scr[prev, c], v_new)
            ms = jnp.mean(o * o, axis=-1, keepdims=True)
            on = o * lax.rsqrt(ms + NORM_EPS) * nw
            o_ref[0, sl, :] = (on * _silu(z_ref[0, sl, :].astype(F32))).astype(o_ref.dtype)
            yield
        s_scr[...] = s

    _alternate((state_chain(), 0), *[(local_work(g), 2 * g) for g in range(GROUPS)])


def _gdn_mixer(p, conv_w, a_log, dt_bias, norm_w, heads, dk, dv):
    bsz, t, _ = p.shape
    tb = min(TIME_BLOCK, t)
    nc = tb // CHUNK
    pipe = _Pipeline(bsz, heads, t // tb)
    qk = heads * dk
    kq = qk // dk
    kv = 2 * qk // dv
    kz = (2 * qk + heads * dv) // dv
    kba = (2 * qk + 2 * heads * dv) // LANE
    taps = conv_w.shape[0]
    smem = pl.BlockSpec(memory_space=pltpu.SMEM)
    in_specs = [
        smem, smem,
        pipe.stream((1, tb, dk), lambda h: h),
        pipe.stream((1, tb, dk), lambda h: kq + h),
        pipe.stream((1, tb, dv), lambda h: kv + h),
        pipe.chain((1, tb, dv), lambda h: kz + h),
        pipe.stream((1, tb, LANE), lambda h: kba),
        pipe.halo(dk, lambda h: h, tb),
        pipe.halo(dk, lambda h: kq + h, tb),
        pipe.halo(dv, lambda h: kv + h, tb),
        pipe.per_head((taps, dk), lambda h: h),
        pipe.per_head((taps, dk), lambda h: kq + h),
        pipe.per_head((taps, dv), lambda h: kv + h),
        pl.BlockSpec((1, dv), lambda l: (0, 0)),
    ]
    return pl.pallas_call(
        functools.partial(_gdn_kernel, pipe=pipe, dk=dk),
        grid=(pipe.steps,),
        in_specs=in_specs,
        out_specs=pipe.chain((1, tb, dv), lambda h: h),
        out_shape=jax.ShapeDtypeStruct((bsz, t, heads * dv), BF16),
        scratch_shapes=[
            pltpu.VMEM((dk, dv), F32),
            pltpu.VMEM((tb + SUBLANE, dk), F32),
            pltpu.VMEM((tb + SUBLANE, dk), F32),
            pltpu.VMEM((tb + SUBLANE, dv), F32),
            pltpu.VMEM((2, nc, CHUNK, dv), F32),
            pltpu.VMEM((2, nc, 2 * CHUNK, dk), BF16),
            pltpu.VMEM((2, nc, CHUNK, CHUNK), BF16),
            pltpu.VMEM((2, nc, CHUNK, dk), BF16),
            pltpu.VMEM((2, nc, SUBLANE, LANE), F32),
        ],
        compiler_params=pltpu.CompilerParams(
            dimension_semantics=("arbitrary",), vmem_limit_bytes=VMEM_LIMIT),
        name="gdn_mixer",
    )(a_log.astype(F32), dt_bias.astype(F32), p, p, p, p, p, p, p, p,
      conv_w, conv_w, conv_w, norm_w.reshape(1, dv))


def _hgrn_kernel(q_ref, f_ref, i_ref, z_ref, lb_ref, nw_ref, o_ref, s_scr):
    t = pl.program_id(2)
    tb = q_ref.shape[1]
    dk = q_ref.shape[2]

    @pl.when(t == 0)
    def _():
        s_scr[...] = jnp.zeros_like(s_scr)

    lb = lb_ref[...]
    q = _silu(q_ref[0].astype(F32))
    f = f_ref[0].astype(F32)
    a1 = jnp.log(lb)
    a2 = jnp.log(1.0 - lb) - _softplus(-f)
    log_f = jnp.maximum(a1, a2) + jnp.log(1.0 + jnp.exp(-jnp.abs(a1 - a2)))
    k = (1.0 - lb) * _sigmoid(-f)
    v = i_ref[0].astype(F32)
    z = z_ref[0].astype(F32)
    nw = nw_ref[...]

    ri, ci = _iotas(CHUNK)
    rows = lax.broadcasted_iota(jnp.int32, (CHUNK, 1), 0)
    gcum_all = _chunk_cumsum(log_f)
    chunks = [slice(c * CHUNK, (c + 1) * CHUNK) for c in range(tb // CHUNK)]
    o_intra, s_add, s_mul, q_dec = [], [], [], []
    for sl in chunks:
        qc, kc, vc = q[sl], k[sl], v[sl]
        gcum = gcum_all[sl]
        g_last = gcum[CHUNK - 1:CHUNK, :]
        blocks = []
        for i in range(CHUNK // SUB):
            lo, hi = i * SUB, (i + 1) * SUB
            ref = gcum[lo - 1:lo, :] if i > 0 else jnp.zeros_like(g_last)
            qs = qc[lo:hi] * jnp.exp(gcum[lo:hi] - ref)
            ks = jnp.where(rows < hi, kc * jnp.exp(ref - gcum), 0.0)
            blocks.append(_mm_nt(qs, ks))
        attn = jnp.where(ri >= ci, jnp.concatenate(blocks, axis=0), 0.0)
        o_intra.append(_mm(attn, vc))
        s_add.append(_mm_tn(kc * jnp.exp(g_last - gcum), vc))
        s_mul.append(_row_to_col(jnp.exp(g_last), dk))
        q_dec.append(qc * jnp.exp(gcum))

    s = s_scr[...]
    for c, sl in enumerate(chunks):
        o = _mm(q_dec[c], s) + o_intra[c]
        s = s * s_mul[c] + s_add[c]
        ms = jnp.mean(o * o, axis=-1, keepdims=True)
        on = o * lax.rsqrt(ms + NORM_EPS) * nw
        o_ref[0, sl, :] = (on * _silu(z[sl])).astype(o_ref.dtype)
    s_scr[...] = s


def _hgrn_mixer(p, lower_bound, norm_w, heads, dk, dv):
    bsz, t, _ = p.shape
    tb = min(TIME_BLOCK, t)
    fdim = heads * dk
    kf = fdim // dk
    ki = 2 * fdim // dv
    kz = (2 * fdim + heads * dv) // dv
    return pl.pallas_call(
        _hgrn_kernel,
        grid=(bsz, heads, t // tb),
        in_specs=[
            pl.BlockSpec((1, tb, dk), lambda b, h, i: (b, i, h)),
            pl.BlockSpec((1, tb, dk), lambda b, h, i: (b, i, kf + h)),
            pl.BlockSpec((1, tb, dv), lambda b, h, i: (b, i, ki + h)),
            pl.BlockSpec((1, tb, dv), lambda b, h, i: (b, i, kz + h)),
            pl.BlockSpec((1, dk), lambda b, h, i: (0, h)),
            pl.BlockSpec((1, dv), lambda b, h, i: (0, 0)),
        ],
        out_specs=pl.BlockSpec((1, tb, dv), lambda b, h, i: (b, i, h)),
        out_shape=jax.ShapeDtypeStruct((bsz, t, heads * dv), BF16),
        scratch_shapes=[pltpu.VMEM((dk, dv), F32)],
        compiler_params=pltpu.CompilerParams(
            dimension_semantics=("parallel", "parallel", "arbitrary"),
            vmem_limit_bytes=VMEM_LIMIT),
        name="hgrn2_mixer",
    )(p, p, p, p, lower_bound.reshape(1, fdim), norm_w.reshape(1, dv))


def _rwkv_kernel(r_ref, k_ref, v_ref, z_ref, wa_ref,
                 rh_ref, kh_ref, vh_ref, zh_ref, wah_ref,
                 mu_r, mu_k, mu_v, mu_z, mu_wa,
                 w0_ref, a0_ref, kk_ref, ka_ref, rk_ref,
                 wl_ref, al_ref, lnw_ref, lnb_ref,
                 o_ref,
                 s_scr, buf_r, buf_k, buf_v, buf_z, buf_wa,
                 wr_scr, u0_scr, arb_scr, y0_scr, bdec_scr, sadd_scr, smul_scr,
                 bonus_scr, gate_scr, *, pipe, dh):
    l = pl.program_id(0)
    _, _, i_local = pipe.decode(pipe.local_block(l))
    tb = r_ref.shape[1]
    lora = wl_ref.shape[0]
    heads = LANE // dh
    n = heads * CHUNK
    slot = lax.rem(l, 2)
    prev = 1 - slot
    rows = tb // GROUPS

    @pl.when(l == 0)
    def _():
        for ref in (s_scr, wr_scr, u0_scr, arb_scr, y0_scr, bdec_scr, sadd_scr, smul_scr,
                    bonus_scr, gate_scr):
            ref[...] = jnp.zeros_like(ref)

    for x_ref, halo_ref, b in ((r_ref, rh_ref, buf_r), (k_ref, kh_ref, buf_k),
                               (v_ref, vh_ref, buf_v), (z_ref, zh_ref, buf_z),
                               (wa_ref, wah_ref, buf_wa)):
        _fill_shift_buffer(b, x_ref, halo_ref, i_local > 0)

    def shifted(b, mu_ref, lo, after):
        cur = b[SUBLANE + lo:SUBLANE + lo + rows, :]
        before = b[SUBLANE - 1 + lo:SUBLANE - 1 + lo + rows, :]
        return cur + (before - cur) * _after(mu_ref[...], after)

    first_matmul = {}

    def local_work(group):
        lo = group * rows
        chunks = [slice(c * CHUNK, (c + 1) * CHUNK) for c in range(rows // CHUNK)]
        after = first_matmul.get(group - 1)
        r = shifted(buf_r, mu_r, lo, after)
        k = shifted(buf_k, mu_k, lo, after)
        v = shifted(buf_v, mu_v, lo, after)
        z = shifted(buf_z, mu_z, lo, after)
        wa = shifted(buf_wa, mu_wa, lo, after)
        wd = wa[:, :lora]
        ad = wa[:, lora:]
        w_log = -_softplus(-(w0_ref[...] + _mm(jnp.tanh(wd), wl_ref[...]))) - 0.5
        log_w = -jnp.exp(w_log)
        a = _sigmoid(a0_ref[...] + _mm(ad, al_ref[...]))
        k2 = k * (1.0 + (a - 1.0) * ka_ref[...])
        kkx = k * kk_ref[...]
        kk = kkx * lax.rsqrt(_head_sum(kkx * kkx, dh) + RWKV_L2_EPS)
        bonus_scr[slot, lo:lo + rows, :] = _head_sum(r * k2 * rk_ref[...], dh) * v
        gate_scr[slot, lo:lo + rows, :] = _silu(z)
        kka = kk * a
        gcum = _chunk_cumsum(log_w)
        e_neg = jnp.exp(-gcum)
        a_t = -kk * jnp.exp(gcum - log_w)
        r_t = r * jnp.exp(gcum)
        b_t = kka * e_neg
        k_t = k2 * e_neg
        ri, ci = _iotas(n)
        same_head = _same_block(ri, ci, CHUNK)
        strict = same_head & (ri > ci)
        incl = same_head & (ri >= ci)
        a_bd = [_head_rows(a_t[sl], dh) for sl in chunks]
        r_bd = [_head_rows(r_t[sl], dh) for sl in chunks]
        v_bd = [_head_rows(v[sl], dh) for sl in chunks]
        yield
        pm = [_mm_nt(jnp.concatenate([a_bd[c], r_bd[c]], axis=0),
                     jnp.concatenate([b_t[sl]] * heads + [k_t[sl]] * heads, axis=0))
              for c, sl in enumerate(chunks)]
        first_matmul[group] = pm[0][0:1, 0:1]
        yield
        a_krk = [jnp.concatenate([jnp.where(strict, m[:n, n:], 0.0),
                                  jnp.where(incl, m[n:, n:], 0.0)], axis=0) for m in pm]
        av = [_mm(x, vb) for x, vb in zip(a_krk, v_bd)]
        tinv = yield from _unit_lower_inverses(
            [jnp.where(strict, -m[:n, :n], 0.0) for m in pm], CHUNK)
        tw = [_mm(ti, jnp.concatenate([ab, x[:n]], axis=1)) for ti, ab, x in zip(tinv, a_bd, av)]
        yield
        for c, sl in enumerate(chunks):
            gc = group * len(chunks) + c
            g_last = gcum[sl.stop - 1:sl.stop, :]
            e_last = jnp.exp(g_last - gcum[sl])
            wr_scr[slot, gc] = jnp.concatenate([tw[c][:, :LANE], r_bd[c]], axis=0).astype(BF16)
            u0_scr[slot, gc] = tw[c][:, LANE:]
            arb_scr[slot, gc] = jnp.where(incl, pm[c][n:, :n], 0.0).astype(BF16)
            y0_scr[slot, gc] = av[c][n:]
            bdec_scr[slot, gc] = _head_rows(kka[sl] * e_last, dh).astype(BF16)
            sadd_scr[slot, gc] = _mm_tn(_head_rows(k2[sl] * e_last, dh), v_bd[c])
            smul_scr[slot, gc] = jnp.broadcast_to(_row_to_col(jnp.exp(g_last), LANE), (LANE, LANE))
        yield

    def state_chain():
        s = jnp.where(lax.rem(l - 1, pipe.nt) == 0, 0.0, s_scr[...])
        for c in range(tb // CHUNK):
            sl = slice(c * CHUNK, (c + 1) * CHUNK)
            rs = _mm(wr_scr[prev, c], s)
            yield
            u = rs[:n] + u0_scr[prev, c]
            y_bd = rs[n:] + _mm(arb_scr[prev, c], u) + y0_scr[prev, c]
            s = s * smul_scr[prev, c] + _mm_tn(bdec_scr[prev, c], u) + sadd_scr[prev, c]
            y = y_bd[:CHUNK]
            for hh in range(1, heads):
                y = y + y_bd[hh * CHUNK:(hh + 1) * CHUNK]
            mean = _head_sum(y, dh) * (1.0 / dh)
            yc = y - mean
            var = _head_sum(yc * yc, dh) * (1.0 / dh)
            yn = yc * lax.rsqrt(var + dh * RWKV_GN_EPS_PER_DIM)
            out = (yn * lnw_ref[...] + lnb_ref[...] + bonus_scr[prev, sl, :]) * gate_scr[prev, sl, :]
            o_ref[0, sl, :] = out.astype(o_ref.dtype)
            yield
        s_scr[...] = s

    _alternate((state_chain(), 0), *[(local_work(g), 2 * g) for g in range(GROUPS)])


def _rwkv_mixer(p, mu, w0, w_lora_up, a0, a_lora_up, k_k, k_a, r_k, ln_w, ln_b, dh):
    bsz, t, _ = p.shape
    branch = w0.shape[0]
    lora = w_lora_up.shape[0]
    assert 2 * lora == LANE and LANE % dh == 0
    tb = min(TIME_BLOCK, t)
    nc = tb // CHUNK
    n = (LANE // dh) * CHUNK
    nb = branch // LANE
    pipe = _Pipeline(bsz, nb, t // tb)
    wa_blk = 4 * nb
    in_specs = (
        [pipe.stream((1, tb, LANE), lambda h, j=j: j * nb + h) for j in range(4)]
        + [pipe.stream((1, tb, LANE), lambda h: wa_blk)]
        + [pipe.halo(LANE, lambda h, j=j: j * nb + h, tb) for j in range(4)]
        + [pipe.halo(LANE, lambda h: wa_blk, tb)]
        + [pipe.per_head((1, LANE), lambda h, j=j: j * nb + h) for j in range(4)]
        + [pipe.per_head((1, LANE), lambda h: wa_blk)]
        + [pipe.per_head((1, LANE), lambda h: h)] * 5
        + [pipe.per_head((lora, LANE), lambda h: h)] * 2
        + [pipe.chain_per_head((1, LANE), lambda h: h)] * 2
    )
    row = lambda a: a.reshape(1, -1)
    return pl.pallas_call(
        functools.partial(_rwkv_kernel, pipe=pipe, dh=dh),
        grid=(pipe.steps,),
        in_specs=in_specs,
        out_specs=pipe.chain((1, tb, LANE), lambda h: h),
        out_shape=jax.ShapeDtypeStruct((bsz, t, branch), BF16),
        scratch_shapes=[
            pltpu.VMEM((LANE, LANE), F32),
        ] + [pltpu.VMEM((tb + SUBLANE, LANE), F32)] * 5 + [
            pltpu.VMEM((2, nc, 2 * n, LANE), BF16),
            pltpu.VMEM((2, nc, n, LANE), F32),
            pltpu.VMEM((2, nc, n, n), BF16),
            pltpu.VMEM((2, nc, n, LANE), F32),
            pltpu.VMEM((2, nc, n, LANE), BF16),
            pltpu.VMEM((2, nc, LANE, LANE), F32),
            pltpu.VMEM((2, nc, LANE, LANE), F32),
            pltpu.VMEM((2, tb, LANE), F32),
            pltpu.VMEM((2, tb, LANE), F32),
        ],
        compiler_params=pltpu.CompilerParams(
            dimension_semantics=("arbitrary",), vmem_limit_bytes=VMEM_LIMIT),
        name="rwkv7_mixer",
    )(p, p, p, p, p, p, p, p, p, p,
      row(mu), row(mu), row(mu), row(mu), row(mu),
      row(w0), row(a0), row(k_k), row(k_a), row(r_k),
      w_lora_up, a_lora_up, row(ln_w), row(ln_b))


def kernel(x, c, ada_w, ada_b, norm_w, gdn_w_in, gdn_conv_w, gdn_a_log, gdn_dt_bias, gdn_norm_w, gdn_w_out, hgrn_w_in, hgrn_lb_logits, hgrn_norm_w, hgrn_w_out, rwkv_w_in, rwkv_mu, rwkv_w0, rwkv_w_lora_up, rwkv_a0, rwkv_a_lora_up, rwkv_k_k, rwkv_k_a, rwkv_r_k, rwkv_ln_w, rwkv_ln_b, rwkv_w_out, final_norm_w):
    depth, d = norm_w.shape
    n_mixers = 3
    gdn_heads = gdn_a_log.shape[1]
    gdn_dv = gdn_norm_w.shape[1]
    gdn_dk = (gdn_conv_w.shape[2] - gdn_heads * gdn_dv) // (2 * gdn_heads)
    hgrn_dv = hgrn_norm_w.shape[1]
    hgrn_f = hgrn_lb_logits.shape[1]
    hgrn_heads = (hgrn_w_in.shape[2] - 2 * hgrn_f) // (2 * hgrn_dv)
    hgrn_dk = hgrn_f // hgrn_heads
    rwkv_dh = rwkv_r_k.shape[2]

    x = x.astype(F32)
    mod = _modulations(c, ada_w, ada_b)
    lower_bounds = _lower_bounds(hgrn_lb_logits)
    for i in range(depth):
        shift, scale, gate = mod[i, :, :d], mod[i, :, d:2 * d], mod[i, :, 2 * d:]
        j = i // n_mixers
        kind = i % n_mixers
        if kind == 0:
            p = _in_projection(x, norm_w[i], scale, shift, gdn_w_in[j])
            og = _gdn_mixer(p, gdn_conv_w[j], gdn_a_log[j], gdn_dt_bias[j], gdn_norm_w[j],
                            gdn_heads, gdn_dk, gdn_dv)
            w_out = gdn_w_out[j]
        elif kind == 1:
            p = _in_projection(x, norm_w[i], scale, shift, hgrn_w_in[j])
            og = _hgrn_mixer(p, lower_bounds[i], hgrn_norm_w[j], hgrn_heads, hgrn_dk, hgrn_dv)
            w_out = hgrn_w_out[j]
        else:
            p = _in_projection(x, norm_w[i], scale, shift, rwkv_w_in[j])
            og = _rwkv_mixer(p, rwkv_mu[j], rwkv_w0[j], rwkv_w_lora_up[j], rwkv_a0[j],
                             rwkv_a_lora_up[j], rwkv_k_k[j], rwkv_k_a[j], rwkv_r_k[j],
                             rwkv_ln_w[j], rwkv_ln_b[j], rwkv_dh)
            w_out = rwkv_w_out[j]
        x = _out_projection(og, w_out, x, gate, final_norm_w, final=(i == depth - 1))
    return x
```

```python
import functools

import jax
import jax.numpy as jnp
from jax import lax
from jax.experimental import pallas as pl
from jax.experimental.pallas import tpu as pltpu

F32 = jnp.float32
BF16 = jnp.bfloat16
HIGHEST = lax.Precision.HIGHEST

NORM_EPS = 1e-6
L2_EPS = 1e-6
RWKV_L2_EPS = 1e-12
RWKV_GN_EPS_PER_DIM = 1e-5

LANE = 128
SUBLANE = 8
HALO_ROWS = 16
MXU_COLS = 256
ROW_TILE = 1024
MAX_COL_TILE = 3072
CHUNK = 64
SUB = 8
TIME_BLOCK = 512
GROUPS = 2
VMEM_LIMIT = 48 * 1024 * 1024


def _dot(a, b):
    return jnp.dot(a, b, preferred_element_type=F32, precision=HIGHEST)


def _mm(a, b):
    return jnp.dot(a.astype(BF16), b.astype(BF16), preferred_element_type=F32)


def _mm_nt(a, b):
    return lax.dot_general(a.astype(BF16), b.astype(BF16), (((1,), (1,)), ((), ())),
                           preferred_element_type=F32)


def _mm_tn(a, b):
    return lax.dot_general(a.astype(BF16), b.astype(BF16), (((0,), (0,)), ((), ())),
                           preferred_element_type=F32)


def _sigmoid(x):
    return 1.0 / (1.0 + jnp.exp(-x))


def _silu(x):
    return x * _sigmoid(x)


def _softplus(x):
    return jnp.maximum(x, 0.0) + jnp.log(1.0 + jnp.exp(-jnp.abs(x)))


def _iotas(n):
    ri = lax.broadcasted_iota(jnp.int32, (n, n), 0)
    ci = lax.broadcasted_iota(jnp.int32, (n, n), 1)
    return ri, ci


def _same_block(ri, ci, b):
    shift = b.bit_length() - 1
    assert 1 << shift == b
    return (ri >> shift) == (ci >> shift)


def _unit_lower_inverses(mats, top):
    n = mats[0].shape[0]
    ri, ci = _iotas(n)
    eye = (ri == ci).astype(F32)
    same = _same_block(ri, ci, SUB)
    nb = [jnp.where(same, a, 0.0) for a in mats]
    nb2 = [_mm(x, x) for x in nb]
    yield
    nb4 = [_mm(x, x) for x in nb2]
    inv = [_mm(eye - x, eye + x2) for x, x2 in zip(nb, nb2)]
    yield
    inv = [_mm(x, eye + x4) for x, x4 in zip(inv, nb4)]
    yield
    b = SUB
    while b < top:
        coupled = _same_block(ri, ci, 2 * b) & jnp.logical_not(_same_block(ri, ci, b))
        off = [jnp.where(coupled, a, 0.0) for a in mats]
        tmp = [_mm(x, o) for x, o in zip(inv, off)]
        yield
        inv = [x - _mm(t, x) for x, t in zip(inv, tmp)]
        yield
        b *= 2
    return inv


def _alternate(*gens_and_delays):
    live = [[g, d] for g, d in gens_and_delays]
    while live:
        for entry in list(live):
            if entry[1] > 0:
                entry[1] -= 1
                continue
            try:
                next(entry[0])
            except StopIteration:
                live.remove(entry)


def _after(x, anchor):
    if anchor is None:
        return x
    return x + jnp.minimum(jnp.abs(anchor), 0.0)


def _chunk_cumsum(x):
    rows, w = x.shape
    ri, ci = _iotas(rows)
    tril = jnp.where(_same_block(ri, ci, CHUNK) & (ri >= ci), 1.0, 0.0).astype(BF16)
    hi = x.astype(BF16)
    rest = x - hi.astype(F32)
    mid = rest.astype(BF16)
    lo = (rest - mid.astype(F32)).astype(BF16)
    parts = jnp.dot(tril, jnp.concatenate([hi, mid, lo], axis=1), preferred_element_type=F32)
    return parts[:, :w] + parts[:, w:2 * w] + parts[:, 2 * w:]


def _head_sum(x, dh):
    lane = lax.broadcasted_iota(jnp.int32, x.shape, 1)
    out = jnp.zeros_like(x)
    for g in range(x.shape[1] // dh):
        in_group = (lane >= g * dh) & (lane < (g + 1) * dh)
        total = jnp.sum(jnp.where(in_group, x, 0.0), axis=-1, keepdims=True)
        out = jnp.where(in_group, total, out)
    return out


def _head_rows(x, dh):
    lane = lax.broadcasted_iota(jnp.int32, x.shape, 1)
    parts = []
    for g in range(x.shape[1] // dh):
        in_group = (lane >= g * dh) & (lane < (g + 1) * dh)
        parts.append(jnp.where(in_group, x, 0.0))
    return jnp.concatenate(parts, axis=0)


def _row_to_col(row, n):
    ri, ci = _iotas(n)
    return jnp.sum(jnp.where(ri == ci, row, 0.0), axis=1, keepdims=True)


def _mod_kernel(c_ref, w_ref, b_ref, o_ref):
    c = c_ref[...]
    o_ref[0] = _dot(_silu(c), w_ref[0]) + b_ref[0]


def _modulations(c, ada_w, ada_b):
    depth, d, d3 = ada_w.shape
    bsz = c.shape[0]
    rows = -(-bsz // SUBLANE) * SUBLANE
    c_pad = jnp.zeros((rows, d), F32).at[:bsz].set(c.astype(F32))
    out = pl.pallas_call(
        _mod_kernel,
        grid=(depth, d3 // d),
        in_specs=[
            pl.BlockSpec((rows, d), lambda i, j: (0, 0)),
            pl.BlockSpec((1, d, d), lambda i, j: (i, 0, j)),
            pl.BlockSpec((1, 1, d), lambda i, j: (i, 0, j)),
        ],
        out_specs=pl.BlockSpec((1, rows, d), lambda i, j: (i, 0, j)),
        out_shape=jax.ShapeDtypeStruct((depth, rows, d3), F32),
        compiler_params=pltpu.CompilerParams(
            dimension_semantics=("parallel", "parallel"), vmem_limit_bytes=VMEM_LIMIT),
        name="adaln_mod",
    )(c_pad, ada_w, ada_b.reshape(depth, 1, d3))
    return out[:, :bsz]


def _lb_kernel(x_ref, o_ref):
    x = x_ref[...]
    depth = x.shape[0]
    m = jnp.max(x, axis=0, keepdims=True)
    e = jnp.exp(x - m)
    p = e / jnp.sum(e, axis=0, keepdims=True)
    acc = jnp.zeros_like(p[0:1])
    for i in range(depth):
        acc = acc + p[i:i + 1]
        o_ref[i:i + 1, :] = acc - p[0:1]


def _lower_bounds(logits):
    return pl.pallas_call(
        _lb_kernel,
        out_shape=jax.ShapeDtypeStruct(logits.shape, F32),
        name="hgrn_lower_bounds",
    )(logits.astype(F32))


def _inproj_kernel(x_ref, nw_ref, sc_ref, sh_ref, w_ref, o_ref, h_scr):
    @pl.when(pl.program_id(2) == 0)
    def _():
        x = x_ref[0]
        ms = jnp.mean(x * x, axis=-1, keepdims=True)
        h = x * lax.rsqrt(ms + NORM_EPS) * nw_ref[...]
        h = h * (1.0 + sc_ref[0]) + sh_ref[0]
        h_scr[...] = h.astype(BF16)

    o_ref[0] = jnp.dot(h_scr[...], w_ref[...], preferred_element_type=F32).astype(o_ref.dtype)


def _col_tile(n):
    best = MXU_COLS
    for k in range(1, n // MXU_COLS + 1):
        if n % (k * MXU_COLS) == 0 and k * MXU_COLS <= MAX_COL_TILE:
            best = k * MXU_COLS
    return best


def _in_projection(x, norm_w, scale, shift, w_in):
    bsz, t, d = x.shape
    n = w_in.shape[1]
    n_pad = -(-n // MXU_COLS) * MXU_COLS
    w = w_in.astype(BF16)
    if n_pad != n:
        w = jnp.pad(w, ((0, 0), (0, n_pad - n)))
    tm = min(ROW_TILE, t)
    tn = _col_tile(n_pad)
    return pl.pallas_call(
        _inproj_kernel,
        grid=(bsz, t // tm, n_pad // tn),
        in_specs=[
            pl.BlockSpec((1, tm, d), lambda b, i, j: (b, i, 0)),
            pl.BlockSpec((1, d), lambda b, i, j: (0, 0)),
            pl.BlockSpec((1, 1, d), lambda b, i, j: (b, 0, 0)),
            pl.BlockSpec((1, 1, d), lambda b, i, j: (b, 0, 0)),
            pl.BlockSpec((d, tn), lambda b, i, j: (0, j)),
        ],
        out_specs=pl.BlockSpec((1, tm, tn), lambda b, i, j: (b, i, j)),
        out_shape=jax.ShapeDtypeStruct((bsz, t, n_pad), BF16),
        scratch_shapes=[pltpu.VMEM((tm, d), BF16)],
        compiler_params=pltpu.CompilerParams(
            dimension_semantics=("parallel", "parallel", "arbitrary"),
            vmem_limit_bytes=VMEM_LIMIT),
        name="norm_inproj",
    )(x, norm_w.reshape(1, d), scale.reshape(bsz, 1, d), shift.reshape(bsz, 1, d), w)


def _outproj_kernel(og_ref, w_ref, x_ref, g_ref, fw_ref, o_ref, *, final):
    y = jnp.dot(og_ref[0], w_ref[...], preferred_element_type=F32)
    xn = x_ref[0] + g_ref[0] * y
    if final:
        ms = jnp.mean(xn * xn, axis=-1, keepdims=True)
        xn = xn * lax.rsqrt(ms + NORM_EPS) * fw_ref[...]
    o_ref[0] = xn


def _out_projection(og, w_out, x, gate, final_w, final):
    bsz, t, d = x.shape
    kdim = og.shape[-1]
    tm = min(ROW_TILE, t)
    return pl.pallas_call(
        functools.partial(_outproj_kernel, final=final),
        grid=(bsz, t // tm),
        in_specs=[
            pl.BlockSpec((1, tm, kdim), lambda b, i: (b, i, 0)),
            pl.BlockSpec((kdim, d), lambda b, i: (0, 0)),
            pl.BlockSpec((1, tm, d), lambda b, i: (b, i, 0)),
            pl.BlockSpec((1, 1, d), lambda b, i: (b, 0, 0)),
            pl.BlockSpec((1, d), lambda b, i: (0, 0)),
        ],
        out_specs=pl.BlockSpec((1, tm, d), lambda b, i: (b, i, 0)),
        out_shape=jax.ShapeDtypeStruct((bsz, t, d), F32),
        compiler_params=pltpu.CompilerParams(
            dimension_semantics=("parallel", "parallel"), vmem_limit_bytes=VMEM_LIMIT),
        name="outproj_residual",
    )(og, w_out.astype(BF16), x, gate.reshape(bsz, 1, d), final_w.reshape(1, d))


def _fill_shift_buffer(buf, x_ref, halo_ref, has_prev):
    halo = halo_ref[0, HALO_ROWS - SUBLANE:, :].astype(F32)
    buf[0:SUBLANE, :] = jnp.where(has_prev, halo, 0.0)
    buf[SUBLANE:, :] = x_ref[0].astype(F32)


class _Pipeline:
    def __init__(self, bsz, heads, nt):
        self.bsz, self.heads, self.nt = bsz, heads, nt
        self.blocks = bsz * heads * nt
        self.steps = self.blocks + 1

    def decode(self, blk):
        return blk // (self.heads * self.nt), (blk // self.nt) % self.heads, blk % self.nt

    def local_block(self, l):
        return jnp.minimum(l, self.blocks - 1)

    def chain_block(self, l):
        return jnp.maximum(l - 1, 0)

    def stream(self, shape, col_fn):
        def index(l):
            b, h, i = self.decode(self.local_block(l))
            return b, i, col_fn(h)
        return pl.BlockSpec(shape, index)

    def halo(self, width, col_fn, tb):
        per = tb // HALO_ROWS

        def index(l):
            b, h, i = self.decode(self.local_block(l))
            return b, jnp.maximum(i * per - 1, 0), col_fn(h)
        return pl.BlockSpec((1, HALO_ROWS, width), index)

    def per_head(self, shape, col_fn):
        def index(l):
            _, h, _ = self.decode(self.local_block(l))
            return 0, col_fn(h)
        return pl.BlockSpec(shape, index)

    def chain(self, shape, col_fn):
        def index(l):
            b, h, i = self.decode(self.chain_block(l))
            return b, i, col_fn(h)
        return pl.BlockSpec(shape, index)

    def chain_per_head(self, shape, col_fn):
        def index(l):
            _, h, _ = self.decode(self.chain_block(l))
            return 0, col_fn(h)
        return pl.BlockSpec(shape, index)


def _gdn_kernel(alog_ref, dtb_ref,
                q_ref, k_ref, v_ref, z_ref, ba_ref,
                qh_ref, kh_ref, vh_ref,
                cq_ref, ck_ref, cv_ref, nw_ref,
                o_ref,
                s_scr, bq, bk, bv, u_scr, wq_scr, attn_scr, kdec_scr, egl_scr, *, pipe, dk):
    l = pl.program_id(0)
    _, h, i_local = pipe.decode(pipe.local_block(l))
    heads = pipe.heads
    tb = q_ref.shape[1]
    dv = v_ref.shape[2]
    taps = cq_ref.shape[0]
    slot = lax.rem(l, 2)
    prev = 1 - slot
    rows = tb // GROUPS

    @pl.when(l == 0)
    def _():
        for ref in (s_scr, u_scr, wq_scr, attn_scr, kdec_scr, egl_scr):
            ref[...] = jnp.zeros_like(ref)

    _fill_shift_buffer(bq, q_ref, qh_ref, i_local > 0)
    _fill_shift_buffer(bk, k_ref, kh_ref, i_local > 0)
    _fill_shift_buffer(bv, v_ref, vh_ref, i_local > 0)

    def conv_silu(buf, cw_ref, lo, after):
        cw = _after(cw_ref[...], after)
        base = SUBLANE - (taps - 1) + lo
        acc = buf[base:base + rows, :] * cw[0:1, :]
        for j in range(1, taps):
            acc = acc + buf[base + j:base + j + rows, :] * cw[j:j + 1, :]
        return _silu(acc)

    first_matmul = {}

    def local_work(group):
        lo = group * rows
        chunks = [slice(c * CHUNK, (c + 1) * CHUNK) for c in range(rows // CHUNK)]
        after = first_matmul.get(group - 1)
        q = conv_silu(bq, cq_ref, lo, after)
        k = conv_silu(bk, ck_ref, lo, after)
        v = conv_silu(bv, cv_ref, lo, after)
        q = q * lax.rsqrt(jnp.sum(q * q, axis=-1, keepdims=True) + L2_EPS) * (dk ** -0.5)
        k = k * lax.rsqrt(jnp.sum(k * k, axis=-1, keepdims=True) + L2_EPS)
        ba = ba_ref[0, lo:lo + rows, :].astype(F32)
        lane = lax.broadcasted_iota(jnp.int32, ba.shape, 1)
        bcol = jnp.sum(jnp.where(lane == h, ba, 0.0), axis=1, keepdims=True)
        acol = jnp.sum(jnp.where(lane == h + heads, ba, 0.0), axis=1, keepdims=True)
        beta = _sigmoid(bcol)
        a_scale = jnp.exp(jnp.full((1, 1), alog_ref[h], F32))
        g = -a_scale * _softplus(acol + dtb_ref[h])
        ri, ci = _iotas(CHUNK)
        incl = ri >= ci
        strict = ri > ci
        gc_col, decay = [], []
        for sl in chunks:
            gcol = g[sl]
            g_row = jnp.sum(jnp.where(ri == ci, gcol, 0.0), axis=0, keepdims=True)
            gc_col.append(jnp.sum(jnp.where(incl, g_row, 0.0), axis=1, keepdims=True))
            gc_row = jnp.sum(jnp.where(ri <= ci, gcol, 0.0), axis=0, keepdims=True)
            decay.append(jnp.where(incl, jnp.exp(gc_col[-1] - gc_row), 0.0))
        kb = [k[sl] * beta[sl] for sl in chunks]
        yield
        kq = [_mm_nt(jnp.concatenate([x, q[sl]], axis=0), k[sl]) for x, sl in zip(kb, chunks)]
        first_matmul[group] = kq[0][0:1, 0:1]
        yield
        tinv = yield from _unit_lower_inverses(
            [jnp.where(strict, m[:CHUNK] * d, 0.0) for m, d in zip(kq, decay)], CHUNK)
        e_gc = [jnp.exp(x) for x in gc_col]
        uw = [_mm(ti, jnp.concatenate([v[sl] * beta[sl], x * e], axis=1))
              for ti, sl, x, e in zip(tinv, chunks, kb, e_gc)]
        yield
        for c, sl in enumerate(chunks):
            gc = group * len(chunks) + c
            g_last = gc_col[c][CHUNK - 1:CHUNK, :]
            u_scr[slot, gc] = uw[c][:, :dv]
            wq_scr[slot, gc] = jnp.concatenate([uw[c][:, dv:], q[sl] * e_gc[c]], axis=0).astype(BF16)
            attn_scr[slot, gc] = (kq[c][CHUNK:] * decay[c]).astype(BF16)
            kdec_scr[slot, gc] = (k[sl] * jnp.exp(g_last - gc_col[c])).astype(BF16)
            egl_scr[slot, gc] = jnp.broadcast_to(jnp.exp(g_last), egl_scr.shape[2:])

    def state_chain():
        nw = nw_ref[...]
        s = jnp.where(lax.rem(l - 1, pipe.nt) == 0, 0.0, s_scr[...])
        for c in range(tb // CHUNK):
            sl = slice(c * CHUNK, (c + 1) * CHUNK)
            ws = _mm(wq_scr[prev, c], s)
            yield
            v_new = u_scr[prev, c] - ws[:CHUNK]
            o = ws[CHUNK:] + _mm(attn_scr[prev, c], v_new)
            s = s * egl_scr[prev, c][0:1, 0:1] + _mm_tn(kdec_scr[prev, c], v_new)
            ms = jnp.mean(o * o, axis=-1, keepdims=True)
            on = o * lax.rsqrt(ms + NORM_EPS) * nw
            o_ref[0, sl, :] = (on * _silu(z_ref[0, sl, :].astype(F32))).astype(o_ref.dtype)
            yield
        s_scr[...] = s

    _alternate((state_chain(), 0), *[(local_work(g), 2 * g) for g in range(GROUPS)])


def _gdn_mixer(p, conv_w, a_log, dt_bias, norm_w, heads, dk, dv):
    bsz, t, _ = p.shape
    tb = min(TIME_BLOCK, t)
    nc = tb // CHUNK
    pipe = _Pipeline(bsz, heads, t // tb)
    qk = heads * dk
    kq = qk // dk
    kv = 2 * qk // dv
    kz = (2 * qk + heads * dv) // dv
    kba = (2 * qk + 2 * heads * dv) // LANE
    taps = conv_w.shape[0]
    smem = pl.BlockSpec(memory_space=pltpu.SMEM)
    in_specs = [
        smem, smem,
        pipe.stream((1, tb, dk), lambda h: h),
        pipe.stream((1, tb, dk), lambda h: kq + h),
        pipe.stream((1, tb, dv), lambda h: kv + h),
        pipe.chain((1, tb, dv), lambda h: kz + h),
        pipe.stream((1, tb, LANE), lambda h: kba),
        pipe.halo(dk, lambda h: h, tb),
        pipe.halo(dk, lambda h: kq + h, tb),
        pipe.halo(dv, lambda h: kv + h, tb),
        pipe.per_head((taps, dk), lambda h: h),
        pipe.per_head((taps, dk), lambda h: kq + h),
        pipe.per_head((taps, dv), lambda h: kv + h),
        pl.BlockSpec((1, dv), lambda l: (0, 0)),
    ]
    return pl.pallas_call(
        functools.partial(_gdn_kernel, pipe=pipe, dk=dk),
        grid=(pipe.steps,),
        in_specs=in_specs,
        out_specs=pipe.chain((1, tb, dv), lambda h: h),
        out_shape=jax.ShapeDtypeStruct((bsz, t, heads * dv), BF16),
        scratch_shapes=[
            pltpu.VMEM((dk, dv), F32),
            pltpu.VMEM((tb + SUBLANE, dk), F32),
            pltpu.VMEM((tb + SUBLANE, dk), F32),
            pltpu.VMEM((tb + SUBLANE, dv), F32),
            pltpu.VMEM((2, nc, CHUNK, dv), F32),
            pltpu.VMEM((2, nc, 2 * CHUNK, dk), BF16),
            pltpu.VMEM((2, nc, CHUNK, CHUNK), BF16),
            pltpu.VMEM((2, nc, CHUNK, dk), BF16),
            pltpu.VMEM((2, nc, SUBLANE, LANE), F32),
        ],
        compiler_params=pltpu.CompilerParams(
            dimension_semantics=("arbitrary",), vmem_limit_bytes=VMEM_LIMIT),
        name="gdn_mixer",
    )(a_log.astype(F32), dt_bias.astype(F32), p, p, p, p, p, p, p, p,
      conv_w, conv_w, conv_w, norm_w.reshape(1, dv))


def _hgrn_kernel(q_ref, f_ref, i_ref, z_ref, lb_ref, nw_ref, o_ref, s_scr):
    t = pl.program_id(2)
    tb = q_ref.shape[1]
    dk = q_ref.shape[2]

    @pl.when(t == 0)
    def _():
        s_scr[...] = jnp.zeros_like(s_scr)

    lb = lb_ref[...]
    q = _silu(q_ref[0].astype(F32))
    f = f_ref[0].astype(F32)
    a1 = jnp.log(lb)
    a2 = jnp.log(1.0 - lb) - _softplus(-f)
    log_f = jnp.maximum(a1, a2) + jnp.log(1.0 + jnp.exp(-jnp.abs(a1 - a2)))
    k = (1.0 - lb) * _sigmoid(-f)
    v = i_ref[0].astype(F32)
    z = z_ref[0].astype(F32)
    nw = nw_ref[...]

    ri, ci = _iotas(CHUNK)
    rows = lax.broadcasted_iota(jnp.int32, (CHUNK, 1), 0)
    gcum_all = _chunk_cumsum(log_f)
    chunks = [slice(c * CHUNK, (c + 1) * CHUNK) for c in range(tb // CHUNK)]
    scores, s_mul, q_dec = [], [], []
    for sl in chunks:
        qc, kc = q[sl], k[sl]
        gcum = gcum_all[sl]
        g_last = gcum[CHUNK - 1:CHUNK, :]
        blocks = []
        for i in range(CHUNK // SUB):
            lo, hi = i * SUB, (i + 1) * SUB
            ref = gcum[lo - 1:lo, :] if i > 0 else jnp.zeros_like(g_last)
            qs = qc[lo:hi] * jnp.exp(gcum[lo:hi] - ref)
            ks = jnp.where(rows < hi, kc * jnp.exp(ref - gcum), 0.0)
            blocks.append(_mm_nt(qs, ks))
        scores.append(blocks)
        s_mul.append(_row_to_col(jnp.exp(g_last), dk))
        q_dec.append(qc * jnp.exp(gcum))
    s_add = [_mm_tn(k[sl] * jnp.exp(gcum_all[sl.stop - 1:sl.stop, :] - gcum_all[sl]), v[sl])
             for sl in chunks]
    o_intra = [_mm(jnp.where(ri >= ci, jnp.concatenate(blocks, axis=0), 0.0), v[sl])
               for blocks, sl in zip(scores, chunks)]

    s = s_scr[...]
    for c, sl in enumerate(chunks):
        o = _mm(q_dec[c], s) + o_intra[c]
        s = s * s_mul[c] + s_add[c]
        ms = jnp.mean(o * o, axis=-1, keepdims=True)
        on = o * lax.rsqrt(ms + NORM_EPS) * nw
        o_ref[0, sl, :] = (on * _silu(z[sl])).astype(o_ref.dtype)
    s_scr[...] = s


def _hgrn_mixer(p, lower_bound, norm_w, heads, dk, dv):
    bsz, t, _ = p.shape
    tb = min(TIME_BLOCK, t)
    fdim = heads * dk
    kf = fdim // dk
    ki = 2 * fdim // dv
    kz = (2 * fdim + heads * dv) // dv
    return pl.pallas_call(
        _hgrn_kernel,
        grid=(bsz, heads, t // tb),
        in_specs=[
            pl.BlockSpec((1, tb, dk), lambda b, h, i: (b, i, h)),
            pl.BlockSpec((1, tb, dk), lambda b, h, i: (b, i, kf + h)),
            pl.BlockSpec((1, tb, dv), lambda b, h, i: (b, i, ki + h)),
            pl.BlockSpec((1, tb, dv), lambda b, h, i: (b, i, kz + h)),
            pl.BlockSpec((1, dk), lambda b, h, i: (0, h)),
            pl.BlockSpec((1, dv), lambda b, h, i: (0, 0)),
        ],
        out_specs=pl.BlockSpec((1, tb, dv), lambda b, h, i: (b, i, h)),
        out_shape=jax.ShapeDtypeStruct((bsz, t, heads * dv), BF16),
        scratch_shapes=[pltpu.VMEM((dk, dv), F32)],
        compiler_params=pltpu.CompilerParams(
            dimension_semantics=("parallel", "parallel", "arbitrary"),
            vmem_limit_bytes=VMEM_LIMIT),
        name="hgrn2_mixer",
    )(p, p, p, p, lower_bound.reshape(1, fdim), norm_w.reshape(1, dv))


def _rwkv_kernel(r_ref, k_ref, v_ref, z_ref, wa_ref,
                 rh_ref, kh_ref, vh_ref, zh_ref, wah_ref,
                 mu_r, mu_k, mu_v, mu_z, mu_wa,
                 w0_ref, a0_ref, kk_ref, ka_ref, rk_ref,
                 wl_ref, al_ref, lnw_ref, lnb_ref,
                 o_ref,
                 s_scr, buf_r, buf_k, buf_v, buf_z, buf_wa,
                 wr_scr, u0_scr, arb_scr, y0_scr, bdec_scr, sadd_scr, smul_scr,
                 bonus_scr, gate_scr, *, pipe, dh):
    l = pl.program_id(0)
    _, _, i_local = pipe.decode(pipe.local_block(l))
    tb = r_ref.shape[1]
    lora = wl_ref.shape[0]
    heads = LANE // dh
    n = heads * CHUNK
    slot = lax.rem(l, 2)
    prev = 1 - slot
    rows = tb // GROUPS

    @pl.when(l == 0)
    def _():
        for ref in (s_scr, wr_scr, u0_scr, arb_scr, y0_scr, bdec_scr, sadd_scr, smul_scr,
                    bonus_scr, gate_scr):
            ref[...] = jnp.zeros_like(ref)

    for x_ref, halo_ref, b in ((r_ref, rh_ref, buf_r), (k_ref, kh_ref, buf_k),
                               (v_ref, vh_ref, buf_v), (z_ref, zh_ref, buf_z),
                               (wa_ref, wah_ref, buf_wa)):
        _fill_shift_buffer(b, x_ref, halo_ref, i_local > 0)

    def shifted(b, mu_ref, lo, after):
        cur = b[SUBLANE + lo:SUBLANE + lo + rows, :]
        before = b[SUBLANE - 1 + lo:SUBLANE - 1 + lo + rows, :]
        return cur + (before - cur) * _after(mu_ref[...], after)

    first_matmul = {}

    def local_work(group):
        lo = group * rows
        chunks = [slice(c * CHUNK, (c + 1) * CHUNK) for c in range(rows // CHUNK)]
        after = first_matmul.get(group - 1)
        r = shifted(buf_r, mu_r, lo, after)
        k = shifted(buf_k, mu_k, lo, after)
        v = shifted(buf_v, mu_v, lo, after)
        z = shifted(buf_z, mu_z, lo, after)
        wa = shifted(buf_wa, mu_wa, lo, after)
        wd = wa[:, :lora]
        ad = wa[:, lora:]
        w_log = -_softplus(-(w0_ref[...] + _mm(jnp.tanh(wd), wl_ref[...]))) - 0.5
        log_w = -jnp.exp(w_log)
        a = _sigmoid(a0_ref[...] + _mm(ad, al_ref[...]))
        k2 = k * (1.0 + (a - 1.0) * ka_ref[...])
        kkx = k * kk_ref[...]
        kk = kkx * lax.rsqrt(_head_sum(kkx * kkx, dh) + RWKV_L2_EPS)
        bonus_scr[slot, lo:lo + rows, :] = _head_sum(r * k2 * rk_ref[...], dh) * v
        gate_scr[slot, lo:lo + rows, :] = _silu(z)
        kka = kk * a
        gcum = _chunk_cumsum(log_w)
        e_neg = jnp.exp(-gcum)
        a_t = -kk * jnp.exp(gcum - log_w)
        r_t = r * jnp.exp(gcum)
        b_t = kka * e_neg
        k_t = k2 * e_neg
        ri, ci = _iotas(n)
        same_head = _same_block(ri, ci, CHUNK)
        strict = same_head & (ri > ci)
        incl = same_head & (ri >= ci)
        a_bd = [_head_rows(a_t[sl], dh) for sl in chunks]
        r_bd = [_head_rows(r_t[sl], dh) for sl in chunks]
        v_bd = [_head_rows(v[sl], dh) for sl in chunks]
        yield
        pm = [_mm_nt(jnp.concatenate([a_bd[c], r_bd[c]], axis=0),
                     jnp.concatenate([b_t[sl]] * heads + [k_t[sl]] * heads, axis=0))
              for c, sl in enumerate(chunks)]
        first_matmul[group] = pm[0][0:1, 0:1]
        yield
        a_krk = [jnp.concatenate([jnp.where(strict, m[:n, n:], 0.0),
                                  jnp.where(incl, m[n:, n:], 0.0)], axis=0) for m in pm]
        av = [_mm(x, vb) for x, vb in zip(a_krk, v_bd)]
        tinv = yield from _unit_lower_inverses(
            [jnp.where(strict, -m[:n, :n], 0.0) for m in pm], CHUNK)
        tw = [_mm(ti, jnp.concatenate([ab, x[:n]], axis=1)) for ti, ab, x in zip(tinv, a_bd, av)]
        yield
        for c, sl in enumerate(chunks):
            gc = group * len(chunks) + c
            g_last = gcum[sl.stop - 1:sl.stop, :]
            e_last = jnp.exp(g_last - gcum[sl])
            wr_scr[slot, gc] = jnp.concatenate([tw[c][:, :LANE], r_bd[c]], axis=0).astype(BF16)
            u0_scr[slot, gc] = tw[c][:, LANE:]
            arb_scr[slot, gc] = jnp.where(incl, pm[c][n:, :n], 0.0).astype(BF16)
            y0_scr[slot, gc] = av[c][n:]
            bdec_scr[slot, gc] = _head_rows(kka[sl] * e_last, dh).astype(BF16)
            sadd_scr[slot, gc] = _mm_tn(_head_rows(k2[sl] * e_last, dh), v_bd[c])
            smul_scr[slot, gc] = jnp.broadcast_to(_row_to_col(jnp.exp(g_last), LANE), (LANE, LANE))
        yield

    def state_chain():
        s = jnp.where(lax.rem(l - 1, pipe.nt) == 0, 0.0, s_scr[...])
        for c in range(tb // CHUNK):
            sl = slice(c * CHUNK, (c + 1) * CHUNK)
            rs = _mm(wr_scr[prev, c], s)
            yield
            u = rs[:n] + u0_scr[prev, c]
            y_bd = rs[n:] + _mm(arb_scr[prev, c], u) + y0_scr[prev, c]
            s = s * smul_scr[prev, c] + _mm_tn(bdec_scr[prev, c], u) + sadd_scr[prev, c]
            y = y_bd[:CHUNK]
            for hh in range(1, heads):
                y = y + y_bd[hh * CHUNK:(hh + 1) * CHUNK]
            mean = _head_sum(y, dh) * (1.0 / dh)
            yc = y - mean
            var = _head_sum(yc * yc, dh) * (1.0 / dh)
            yn = yc * lax.rsqrt(var + dh * RWKV_GN_EPS_PER_DIM)
            out = (yn * lnw_ref[...] + lnb_ref[...] + bonus_scr[prev, sl, :]) * gate_scr[prev, sl, :]
            o_ref[0, sl, :] = out.astype(o_ref.dtype)
            yield
        s_scr[...] = s

    _alternate((state_chain(), 0), *[(local_work(g), 2 * g) for g in range(GROUPS)])


def _rwkv_mixer(p, mu, w0, w_lora_up, a0, a_lora_up, k_k, k_a, r_k, ln_w, ln_b, dh):
    bsz, t, _ = p.shape
    branch = w0.shape[0]
    lora = w_lora_up.shape[0]
    assert 2 * lora == LANE and LANE % dh == 0
    tb = min(TIME_BLOCK, t)
    nc = tb // CHUNK
    n = (LANE // dh) * CHUNK
    nb = branch // LANE
    pipe = _Pipeline(bsz, nb, t // tb)
    wa_blk = 4 * nb
    in_specs = (
        [pipe.stream((1, tb, LANE), lambda h, j=j: j * nb + h) for j in range(4)]
        + [pipe.stream((1, tb, LANE), lambda h: wa_blk)]
        + [pipe.halo(LANE, lambda h, j=j: j * nb + h, tb) for j in range(4)]
        + [pipe.halo(LANE, lambda h: wa_blk, tb)]
        + [pipe.per_head((1, LANE), lambda h, j=j: j * nb + h) for j in range(4)]
        + [pipe.per_head((1, LANE), lambda h: wa_blk)]
        + [pipe.per_head((1, LANE), lambda h: h)] * 5
        + [pipe.per_head((lora, LANE), lambda h: h)] * 2
        + [pipe.chain_per_head((1, LANE), lambda h: h)] * 2
    )
    row = lambda a: a.reshape(1, -1)
    return pl.pallas_call(
        functools.partial(_rwkv_kernel, pipe=pipe, dh=dh),
        grid=(pipe.steps,),
        in_specs=in_specs,
        out_specs=pipe.chain((1, tb, LANE), lambda h: h),
        out_shape=jax.ShapeDtypeStruct((bsz, t, branch), BF16),
        scratch_shapes=[
            pltpu.VMEM((LANE, LANE), F32),
        ] + [pltpu.VMEM((tb + SUBLANE, LANE), F32)] * 5 + [
            pltpu.VMEM((2, nc, 2 * n, LANE), BF16),
            pltpu.VMEM((2, nc, n, LANE), F32),
            pltpu.VMEM((2, nc, n, n), BF16),
            pltpu.VMEM((2, nc, n, LANE), F32),
            pltpu.VMEM((2, nc, n, LANE), BF16),
            pltpu.VMEM((2, nc, LANE, LANE), F32),
            pltpu.VMEM((2, nc, LANE, LANE), F32),
            pltpu.VMEM((2, tb, LANE), F32),
            pltpu.VMEM((2, tb, LANE), F32),
        ],
        compiler_params=pltpu.CompilerParams(
            dimension_semantics=("arbitrary",), vmem_limit_bytes=VMEM_LIMIT),
        name="rwkv7_mixer",
    )(p, p, p, p, p, p, p, p, p, p,
      row(mu), row(mu), row(mu), row(mu), row(mu),
      row(w0), row(a0), row(k_k), row(k_a), row(r_k),
      w_lora_up, a_lora_up, row(ln_w), row(ln_b))


def kernel(x, c, ada_w, ada_b, norm_w, gdn_w_in, gdn_conv_w, gdn_a_log, gdn_dt_bias, gdn_norm_w, gdn_w_out, hgrn_w_in, hgrn_lb_logits, hgrn_norm_w, hgrn_w_out, rwkv_w_in, rwkv_mu, rwkv_w0, rwkv_w_lora_up, rwkv_a0, rwkv_a_lora_up, rwkv_k_k, rwkv_k_a, rwkv_r_k, rwkv_ln_w, rwkv_ln_b, rwkv_w_out, final_norm_w):
    depth, d = norm_w.shape
    n_mixers = 3
    gdn_heads = gdn_a_log.shape[1]
    gdn_dv = gdn_norm_w.shape[1]
    gdn_dk = (gdn_conv_w.shape[2] - gdn_heads * gdn_dv) // (2 * gdn_heads)
    hgrn_dv = hgrn_norm_w.shape[1]
    hgrn_f = hgrn_lb_logits.shape[1]
    hgrn_heads = (hgrn_w_in.shape[2] - 2 * hgrn_f) // (2 * hgrn_dv)
    hgrn_dk = hgrn_f // hgrn_heads
    rwkv_dh = rwkv_r_k.shape[2]

    x = x.astype(F32)
    mod = _modulations(c, ada_w, ada_b)
    lower_bounds = _lower_bounds(hgrn_lb_logits)
    for i in range(depth):
        shift, scale, gate = mod[i, :, :d], mod[i, :, d:2 * d], mod[i, :, 2 * d:]
        j = i // n_mixers
        kind = i % n_mixers
        if kind == 0:
            p = _in_projection(x, norm_w[i], scale, shift, gdn_w_in[j])
            og = _gdn_mixer(p, gdn_conv_w[j], gdn_a_log[j], gdn_dt_bias[j], gdn_norm_w[j],
                            gdn_heads, gdn_dk, gdn_dv)
            w_out = gdn_w_out[j]
        elif kind == 1:
            p = _in_projection(x, norm_w[i], scale, shift, hgrn_w_in[j])
            og = _hgrn_mixer(p, lower_bounds[i], hgrn_norm_w[j], hgrn_heads, hgrn_dk, hgrn_dv)
            w_out = hgrn_w_out[j]
        else:
            p = _in_projection(x, norm_w[i], scale, shift, rwkv_w_in[j])
            og = _rwkv_mixer(p, rwkv_mu[j], rwkv_w0[j], rwkv_w_lora_up[j], rwkv_a0[j],
                             rwkv_a_lora_up[j], rwkv_k_k[j], rwkv_k_a[j], rwkv_r_k[j],
                             rwkv_ln_w[j], rwkv_ln_b[j], rwkv_dh)
            w_out = rwkv_w_out[j]
        x = _out_projection(og, w_out, x, gate, final_norm_w, final=(i == depth - 1))
    return x
```

```python
import functools

import jax
import jax.numpy as jnp
from jax import lax
from jax.experimental import pallas as pl
from jax.experimental.pallas import tpu as pltpu

F32 = jnp.float32
BF16 = jnp.bfloat16
HIGHEST = lax.Precision.HIGHEST

NORM_EPS = 1e-6
L2_EPS = 1e-6
RWKV_L2_EPS = 1e-12
RWKV_GN_EPS_PER_DIM = 1e-5

LANE = 128
SUBLANE = 8
HALO_ROWS = 16
MXU_COLS = 256
ROW_TILE = 1024
MAX_COL_TILE = 3072
CHUNK = 64
SUB = 8
TIME_BLOCK = 512
GROUPS = 2
VMEM_LIMIT = 48 * 1024 * 1024


def _dot(a, b):
    return jnp.dot(a, b, preferred_element_type=F32, precision=HIGHEST)


def _mm(a, b):
    return jnp.dot(a.astype(BF16), b.astype(BF16), preferred_element_type=F32)


def _mm_nt(a, b):
    return lax.dot_general(a.astype(BF16), b.astype(BF16), (((1,), (1,)), ((), ())),
                           preferred_element_type=F32)


def _mm_tn(a, b):
    return lax.dot_general(a.astype(BF16), b.astype(BF16), (((0,), (0,)), ((), ())),
                           preferred_element_type=F32)


def _sigmoid(x):
    return 1.0 / (1.0 + jnp.exp(-x))


def _silu(x):
    return x * _sigmoid(x)


def _softplus(x):
    return jnp.maximum(x, 0.0) + jnp.log(1.0 + jnp.exp(-jnp.abs(x)))


def _iotas(n):
    ri = lax.broadcasted_iota(jnp.int32, (n, n), 0)
    ci = lax.broadcasted_iota(jnp.int32, (n, n), 1)
    return ri, ci


def _same_block(ri, ci, b):
    shift = b.bit_length() - 1
    assert 1 << shift == b
    return (ri >> shift) == (ci >> shift)


def _unit_lower_inverses(mats, top):
    n = mats[0].shape[0]
    ri, ci = _iotas(n)
    eye = (ri == ci).astype(F32)
    same = _same_block(ri, ci, SUB)
    nb = [jnp.where(same, a, 0.0) for a in mats]
    nb2 = [_mm(x, x) for x in nb]
    yield
    nb4 = [_mm(x, x) for x in nb2]
    inv = [_mm(eye - x, eye + x2) for x, x2 in zip(nb, nb2)]
    yield
    inv = [_mm(x, eye + x4) for x, x4 in zip(inv, nb4)]
    yield
    b = SUB
    while b < top:
        coupled = _same_block(ri, ci, 2 * b) & jnp.logical_not(_same_block(ri, ci, b))
        off = [jnp.where(coupled, a, 0.0) for a in mats]
        tmp = [_mm(x, o) for x, o in zip(inv, off)]
        yield
        inv = [x - _mm(t, x) for x, t in zip(inv, tmp)]
        yield
        b *= 2
    return inv


def _alternate(*gens_and_delays):
    live = [[g, d] for g, d in gens_and_delays]
    while live:
        for entry in list(live):
            if entry[1] > 0:
                entry[1] -= 1
                continue
            try:
                next(entry[0])
            except StopIteration:
                live.remove(entry)


def _after(x, anchor):
    if anchor is None:
        return x
    return x + jnp.minimum(jnp.abs(anchor), 0.0)


def _chunk_cumsum(x):
    rows, w = x.shape
    ri, ci = _iotas(rows)
    tril = jnp.where(_same_block(ri, ci, CHUNK) & (ri >= ci), 1.0, 0.0).astype(BF16)
    hi = x.astype(BF16)
    rest = x - hi.astype(F32)
    mid = rest.astype(BF16)
    lo = (rest - mid.astype(F32)).astype(BF16)
    parts = jnp.dot(tril, jnp.concatenate([hi, mid, lo], axis=1), preferred_element_type=F32)
    return parts[:, :w] + parts[:, w:2 * w] + parts[:, 2 * w:]


def _head_sum(x, dh):
    lane = lax.broadcasted_iota(jnp.int32, x.shape, 1)
    out = jnp.zeros_like(x)
    for g in range(x.shape[1] // dh):
        in_group = (lane >= g * dh) & (lane < (g + 1) * dh)
        total = jnp.sum(jnp.where(in_group, x, 0.0), axis=-1, keepdims=True)
        out = jnp.where(in_group, total, out)
    return out


def _head_rows(x, dh):
    x = x.astype(BF16)
    lane = lax.broadcasted_iota(jnp.int32, x.shape, 1)
    parts = []
    for g in range(x.shape[1] // dh):
        in_group = (lane >= g * dh) & (lane < (g + 1) * dh)
        parts.append(jnp.where(in_group, x, jnp.zeros_like(x)))
    return jnp.concatenate(parts, axis=0)


def _row_to_col(row, n):
    ri, ci = _iotas(n)
    return jnp.sum(jnp.where(ri == ci, row, 0.0), axis=1, keepdims=True)


def _mod_kernel(c_ref, w_ref, b_ref, o_ref):
    c = c_ref[...]
    o_ref[0] = _dot(_silu(c), w_ref[0]) + b_ref[0]


def _modulations(c, ada_w, ada_b):
    depth, d, d3 = ada_w.shape
    bsz = c.shape[0]
    rows = -(-bsz // SUBLANE) * SUBLANE
    c_pad = jnp.zeros((rows, d), F32).at[:bsz].set(c.astype(F32))
    out = pl.pallas_call(
        _mod_kernel,
        grid=(depth, d3 // d),
        in_specs=[
            pl.BlockSpec((rows, d), lambda i, j: (0, 0)),
            pl.BlockSpec((1, d, d), lambda i, j: (i, 0, j)),
            pl.BlockSpec((1, 1, d), lambda i, j: (i, 0, j)),
        ],
        out_specs=pl.BlockSpec((1, rows, d), lambda i, j: (i, 0, j)),
        out_shape=jax.ShapeDtypeStruct((depth, rows, d3), F32),
        compiler_params=pltpu.CompilerParams(
            dimension_semantics=("parallel", "parallel"), vmem_limit_bytes=VMEM_LIMIT),
        name="adaln_mod",
    )(c_pad, ada_w, ada_b.reshape(depth, 1, d3))
    return out[:, :bsz]


def _lb_kernel(x_ref, o_ref):
    x = x_ref[...]
    depth = x.shape[0]
    m = jnp.max(x, axis=0, keepdims=True)
    e = jnp.exp(x - m)
    p = e / jnp.sum(e, axis=0, keepdims=True)
    acc = jnp.zeros_like(p[0:1])
    for i in range(depth):
        acc = acc + p[i:i + 1]
        o_ref[i:i + 1, :] = acc - p[0:1]


def _lower_bounds(logits):
    return pl.pallas_call(
        _lb_kernel,
        out_shape=jax.ShapeDtypeStruct(logits.shape, F32),
        name="hgrn_lower_bounds",
    )(logits.astype(F32))


def _inproj_kernel(x_ref, nw_ref, sc_ref, sh_ref, w_ref, o_ref, h_scr):
    @pl.when(pl.program_id(2) == 0)
    def _():
        x = x_ref[0]
        ms = jnp.mean(x * x, axis=-1, keepdims=True)
        h = x * lax.rsqrt(ms + NORM_EPS) * nw_ref[...]
        h = h * (1.0 + sc_ref[0]) + sh_ref[0]
        h_scr[...] = h.astype(BF16)

    o_ref[0] = jnp.dot(h_scr[...], w_ref[...], preferred_element_type=F32).astype(o_ref.dtype)


def _col_tile(n):
    best = MXU_COLS
    for k in range(1, n // MXU_COLS + 1):
        if n % (k * MXU_COLS) == 0 and k * MXU_COLS <= MAX_COL_TILE:
            best = k * MXU_COLS
    return best


def _in_projection(x, norm_w, scale, shift, w_in):
    bsz, t, d = x.shape
    n = w_in.shape[1]
    n_pad = -(-n // MXU_COLS) * MXU_COLS
    w = w_in.astype(BF16)
    if n_pad != n:
        w = jnp.pad(w, ((0, 0), (0, n_pad - n)))
    tm = min(ROW_TILE, t)
    tn = _col_tile(n_pad)
    return pl.pallas_call(
        _inproj_kernel,
        grid=(bsz, t // tm, n_pad // tn),
        in_specs=[
            pl.BlockSpec((1, tm, d), lambda b, i, j: (b, i, 0)),
            pl.BlockSpec((1, d), lambda b, i, j: (0, 0)),
            pl.BlockSpec((1, 1, d), lambda b, i, j: (b, 0, 0)),
            pl.BlockSpec((1, 1, d), lambda b, i, j: (b, 0, 0)),
            pl.BlockSpec((d, tn), lambda b, i, j: (0, j)),
        ],
        out_specs=pl.BlockSpec((1, tm, tn), lambda b, i, j: (b, i, j)),
        out_shape=jax.ShapeDtypeStruct((bsz, t, n_pad), BF16),
        scratch_shapes=[pltpu.VMEM((tm, d), BF16)],
        compiler_params=pltpu.CompilerParams(
            dimension_semantics=("parallel", "parallel", "arbitrary"),
            vmem_limit_bytes=VMEM_LIMIT),
        name="norm_inproj",
    )(x, norm_w.reshape(1, d), scale.reshape(bsz, 1, d), shift.reshape(bsz, 1, d), w)


def _outproj_kernel(og_ref, w_ref, x_ref, g_ref, fw_ref, o_ref, *, final):
    y = jnp.dot(og_ref[0], w_ref[...], preferred_element_type=F32)
    xn = x_ref[0] + g_ref[0] * y
    if final:
        ms = jnp.mean(xn * xn, axis=-1, keepdims=True)
        xn = xn * lax.rsqrt(ms + NORM_EPS) * fw_ref[...]
    o_ref[0] = xn


def _out_projection(og, w_out, x, gate, final_w, final):
    bsz, t, d = x.shape
    kdim = og.shape[-1]
    tm = min(ROW_TILE, t)
    return pl.pallas_call(
        functools.partial(_outproj_kernel, final=final),
        grid=(bsz, t // tm),
        in_specs=[
            pl.BlockSpec((1, tm, kdim), lambda b, i: (b, i, 0)),
            pl.BlockSpec((kdim, d), lambda b, i: (0, 0)),
            pl.BlockSpec((1, tm, d), lambda b, i: (b, i, 0)),
            pl.BlockSpec((1, 1, d), lambda b, i: (b, 0, 0)),
            pl.BlockSpec((1, d), lambda b, i: (0, 0)),
        ],
        out_specs=pl.BlockSpec((1, tm, d), lambda b, i: (b, i, 0)),
        out_shape=jax.ShapeDtypeStruct((bsz, t, d), F32),
        compiler_params=pltpu.CompilerParams(
            dimension_semantics=("parallel", "parallel"), vmem_limit_bytes=VMEM_LIMIT),
        name="outproj_residual",
    )(og, w_out.astype(BF16), x, gate.reshape(bsz, 1, d), final_w.reshape(1, d))


def _fill_shift_buffer(buf, x_ref, halo_ref, has_prev):
    halo = halo_ref[0, HALO_ROWS - SUBLANE:, :].astype(F32)
    buf[0:SUBLANE, :] = jnp.where(has_prev, halo, 0.0)
    buf[SUBLANE:, :] = x_ref[0].astype(F32)


class _Pipeline:
    def __init__(self, bsz, heads, nt):
        self.bsz, self.heads, self.nt = bsz, heads, nt
        self.blocks = bsz * heads * nt
        self.steps = self.blocks + 1

    def decode(self, blk):
        return blk // (self.heads * self.nt), (blk // self.nt) % self.heads, blk % self.nt

    def local_block(self, l):
        return jnp.minimum(l, self.blocks - 1)

    def chain_block(self, l):
        return jnp.maximum(l - 1, 0)

    def stream(self, shape, col_fn):
        def index(l):
            b, h, i = self.decode(self.local_block(l))
            return b, i, col_fn(h)
        return pl.BlockSpec(shape, index)

    def halo(self, width, col_fn, tb):
        per = tb // HALO_ROWS

        def index(l):
            b, h, i = self.decode(self.local_block(l))
            return b, jnp.maximum(i * per - 1, 0), col_fn(h)
        return pl.BlockSpec((1, HALO_ROWS, width), index)

    def per_head(self, shape, col_fn):
        def index(l):
            _, h, _ = self.decode(self.local_block(l))
            return 0, col_fn(h)
        return pl.BlockSpec(shape, index)

    def chain(self, shape, col_fn):
        def index(l):
            b, h, i = self.decode(self.chain_block(l))
            return b, i, col_fn(h)
        return pl.BlockSpec(shape, index)

    def chain_per_head(self, shape, col_fn):
        def index(l):
            _, h, _ = self.decode(self.chain_block(l))
            return 0, col_fn(h)
        return pl.BlockSpec(shape, index)


def _gdn_kernel(alog_ref, dtb_ref,
                q_ref, k_ref, v_ref, z_ref, ba_ref,
                qh_ref, kh_ref, vh_ref,
                cq_ref, ck_ref, cv_ref, nw_ref,
                o_ref,
                s_scr, bq, bk, bv, u_scr, wq_scr, attn_scr, kdec_scr, egl_scr, *, pipe, dk):
    l = pl.program_id(0)
    _, h, i_local = pipe.decode(pipe.local_block(l))
    heads = pipe.heads
    tb = q_ref.shape[1]
    dv = v_ref.shape[2]
    taps = cq_ref.shape[0]
    slot = lax.rem(l, 2)
    prev = 1 - slot
    rows = tb // GROUPS

    @pl.when(l == 0)
    def _():
        for ref in (s_scr, u_scr, wq_scr, attn_scr, kdec_scr, egl_scr):
            ref[...] = jnp.zeros_like(ref)

    _fill_shift_buffer(bq, q_ref, qh_ref, i_local > 0)
    _fill_shift_buffer(bk, k_ref, kh_ref, i_local > 0)
    _fill_shift_buffer(bv, v_ref, vh_ref, i_local > 0)

    def conv_silu(buf, cw_ref, lo, after):
        cw = _after(cw_ref[...], after)
        base = SUBLANE - (taps - 1) + lo
        acc = buf[base:base + rows, :] * cw[0:1, :]
        for j in range(1, taps):
            acc = acc + buf[base + j:base + j + rows, :] * cw[j:j + 1, :]
        return _silu(acc)

    first_matmul = {}

    def local_work(group):
        lo = group * rows
        chunks = [slice(c * CHUNK, (c + 1) * CHUNK) for c in range(rows // CHUNK)]
        after = first_matmul.get(group - 1)
        q = conv_silu(bq, cq_ref, lo, after)
        k = conv_silu(bk, ck_ref, lo, after)
        v = conv_silu(bv, cv_ref, lo, after)
        q = q * lax.rsqrt(jnp.sum(q * q, axis=-1, keepdims=True) + L2_EPS) * (dk ** -0.5)
        k = k * lax.rsqrt(jnp.sum(k * k, axis=-1, keepdims=True) + L2_EPS)
        ba = ba_ref[0, lo:lo + rows, :].astype(F32)
        lane = lax.broadcasted_iota(jnp.int32, ba.shape, 1)
        bcol = jnp.sum(jnp.where(lane == h, ba, 0.0), axis=1, keepdims=True)
        acol = jnp.sum(jnp.where(lane == h + heads, ba, 0.0), axis=1, keepdims=True)
        beta = _sigmoid(bcol)
        a_scale = jnp.exp(jnp.full((1, 1), alog_ref[h], F32))
        g = -a_scale * _softplus(acol + dtb_ref[h])
        ri, ci = _iotas(CHUNK)
        incl = ri >= ci
        strict = ri > ci
        gc_col, decay = [], []
        for sl in chunks:
            gcol = g[sl]
            g_row = jnp.sum(jnp.where(ri == ci, gcol, 0.0), axis=0, keepdims=True)
            gc_col.append(jnp.sum(jnp.where(incl, g_row, 0.0), axis=1, keepdims=True))
            gc_row = jnp.sum(jnp.where(ri <= ci, gcol, 0.0), axis=0, keepdims=True)
            decay.append(jnp.where(incl, jnp.exp(gc_col[-1] - gc_row), 0.0))
        kb = [k[sl] * beta[sl] for sl in chunks]
        yield
        kq = [_mm_nt(jnp.concatenate([x, q[sl]], axis=0), k[sl]) for x, sl in zip(kb, chunks)]
        first_matmul[group] = kq[0][0:1, 0:1]
        yield
        tinv = yield from _unit_lower_inverses(
            [jnp.where(strict, m[:CHUNK] * d, 0.0) for m, d in zip(kq, decay)], CHUNK)
        e_gc = [jnp.exp(x) for x in gc_col]
        uw = [_mm(ti, jnp.concatenate([v[sl] * beta[sl], x * e], axis=1))
              for ti, sl, x, e in zip(tinv, chunks, kb, e_gc)]
        yield
        for c, sl in enumerate(chunks):
            gc = group * len(chunks) + c
            g_last = gc_col[c][CHUNK - 1:CHUNK, :]
            u_scr[slot, gc] = uw[c][:, :dv]
            wq_scr[slot, gc] = jnp.concatenate([uw[c][:, dv:], q[sl] * e_gc[c]], axis=0).astype(BF16)
            attn_scr[slot, gc] = (kq[c][CHUNK:] * decay[c]).astype(BF16)
            kdec_scr[slot, gc] = (k[sl] * jnp.exp(g_last - gc_col[c])).astype(BF16)
            egl_scr[slot, gc] = jnp.broadcast_to(jnp.exp(g_last), egl_scr.shape[2:])

    def state_chain():
        nw = nw_ref[...]
        s = jnp.where(lax.rem(l - 1, pipe.nt) == 0, 0.0, s_scr[...])
        for c in range(tb // CHUNK):
            sl = slice(c * CHUNK, (c + 1) * CHUNK)
            ws = _mm(wq_scr[prev, c], s)
            yield
            v_new = u_scr[prev, c] - ws[:CHUNK]
            o = ws[CHUNK:] + _mm(attn_scr[prev, c], v_new)
            s = s * egl_scr[prev, c][0:1, 0:1] + _mm_tn(kdec_scr[prev, c], v_new)
            ms = jnp.mean(o * o, axis=-1, keepdims=True)
            on = o * lax.rsqrt(ms + NORM_EPS) * nw
            o_ref[0, sl, :] = (on * _silu(z_ref[0, sl, :].astype(F32))).astype(o_ref.dtype)
            yield
        s_scr[...] = s

    _alternate((state_chain(), 0), *[(local_work(g), 2 * g) for g in range(GROUPS)])


def _gdn_mixer(p, conv_w, a_log, dt_bias, norm_w, heads, dk, dv):
    bsz, t, _ = p.shape
    tb = min(TIME_BLOCK, t)
    nc = tb // CHUNK
    pipe = _Pipeline(bsz, heads, t // tb)
    qk = heads * dk
    kq = qk // dk
    kv = 2 * qk // dv
    kz = (2 * qk + heads * dv) // dv
    kba = (2 * qk + 2 * heads * dv) // LANE
    taps = conv_w.shape[0]
    smem = pl.BlockSpec(memory_space=pltpu.SMEM)
    in_specs = [
        smem, smem,
        pipe.stream((1, tb, dk), lambda h: h),
        pipe.stream((1, tb, dk), lambda h: kq + h),
        pipe.stream((1, tb, dv), lambda h: kv + h),
        pipe.chain((1, tb, dv), lambda h: kz + h),
        pipe.stream((1, tb, LANE), lambda h: kba),
        pipe.halo(dk, lambda h: h, tb),
        pipe.halo(dk, lambda h: kq + h, tb),
        pipe.halo(dv, lambda h: kv + h, tb),
        pipe.per_head((taps, dk), lambda h: h),
        pipe.per_head((taps, dk), lambda h: kq + h),
        pipe.per_head((taps, dv), lambda h: kv + h),
        pl.BlockSpec((1, dv), lambda l: (0, 0)),
    ]
    return pl.pallas_call(
        functools.partial(_gdn_kernel, pipe=pipe, dk=dk),
        grid=(pipe.steps,),
        in_specs=in_specs,
        out_specs=pipe.chain((1, tb, dv), lambda h: h),
        out_shape=jax.ShapeDtypeStruct((bsz, t, heads * dv), BF16),
        scratch_shapes=[
            pltpu.VMEM((dk, dv), F32),
            pltpu.VMEM((tb + SUBLANE, dk), F32),
            pltpu.VMEM((tb + SUBLANE, dk), F32),
            pltpu.VMEM((tb + SUBLANE, dv), F32),
            pltpu.VMEM((2, nc, CHUNK, dv), F32),
            pltpu.VMEM((2, nc, 2 * CHUNK, dk), BF16),
            pltpu.VMEM((2, nc, CHUNK, CHUNK), BF16),
            pltpu.VMEM((2, nc, CHUNK, dk), BF16),
            pltpu.VMEM((2, nc, SUBLANE, LANE), F32),
        ],
        compiler_params=pltpu.CompilerParams(
            dimension_semantics=("arbitrary",), vmem_limit_bytes=VMEM_LIMIT),
        name="gdn_mixer",
    )(a_log.astype(F32), dt_bias.astype(F32), p, p, p, p, p, p, p, p,
      conv_w, conv_w, conv_w, norm_w.reshape(1, dv))


def _hgrn_kernel(q_ref, f_ref, i_ref, z_ref, lb_ref, nw_ref, o_ref, s_scr):
    t = pl.program_id(2)
    tb = q_ref.shape[1]
    dk = q_ref.shape[2]

    @pl.when(t == 0)
    def _():
        s_scr[...] = jnp.zeros_like(s_scr)

    lb = lb_ref[...]
    q = _silu(q_ref[0].astype(F32))
    f = f_ref[0].astype(F32)
    a1 = jnp.log(lb)
    a2 = jnp.log(1.0 - lb) - _softplus(-f)
    log_f = jnp.maximum(a1, a2) + jnp.log(1.0 + jnp.exp(-jnp.abs(a1 - a2)))
    k = (1.0 - lb) * _sigmoid(-f)
    v = i_ref[0].astype(F32)
    z = z_ref[0].astype(F32)
    nw = nw_ref[...]

    ri, ci = _iotas(CHUNK)
    rows = lax.broadcasted_iota(jnp.int32, (CHUNK, 1), 0)
    gcum_all = _chunk_cumsum(log_f)
    chunks = [slice(c * CHUNK, (c + 1) * CHUNK) for c in range(tb // CHUNK)]
    scores, s_mul, q_dec = [], [], []
    for sl in chunks:
        qc, kc = q[sl], k[sl]
        gcum = gcum_all[sl]
        g_last = gcum[CHUNK - 1:CHUNK, :]
        blocks = []
        for i in range(CHUNK // SUB):
            lo, hi = i * SUB, (i + 1) * SUB
            ref = gcum[lo - 1:lo, :] if i > 0 else jnp.zeros_like(g_last)
            qs = qc[lo:hi] * jnp.exp(gcum[lo:hi] - ref)
            ks = jnp.where(rows < hi, kc * jnp.exp(ref - gcum), 0.0)
            blocks.append(_mm_nt(qs, ks))
        scores.append(blocks)
        s_mul.append(_row_to_col(jnp.exp(g_last), dk))
        q_dec.append(qc * jnp.exp(gcum))
    s_add = [_mm_tn(k[sl] * jnp.exp(gcum_all[sl.stop - 1:sl.stop, :] - gcum_all[sl]), v[sl])
             for sl in chunks]
    o_intra = [_mm(jnp.where(ri >= ci, jnp.concatenate(blocks, axis=0), 0.0), v[sl])
               for blocks, sl in zip(scores, chunks)]

    s = s_scr[...]
    for c, sl in enumerate(chunks):
        o = _mm(q_dec[c], s) + o_intra[c]
        s = s * s_mul[c] + s_add[c]
        ms = jnp.mean(o * o, axis=-1, keepdims=True)
        on = o * lax.rsqrt(ms + NORM_EPS) * nw
        o_ref[0, sl, :] = (on * _silu(z[sl])).astype(o_ref.dtype)
    s_scr[...] = s


def _hgrn_mixer(p, lower_bound, norm_w, heads, dk, dv):
    bsz, t, _ = p.shape
    tb = min(TIME_BLOCK, t)
    fdim = heads * dk
    kf = fdim // dk
    ki = 2 * fdim // dv
    kz = (2 * fdim + heads * dv) // dv
    return pl.pallas_call(
        _hgrn_kernel,
        grid=(bsz, heads, t // tb),
        in_specs=[
            pl.BlockSpec((1, tb, dk), lambda b, h, i: (b, i, h)),
            pl.BlockSpec((1, tb, dk), lambda b, h, i: (b, i, kf + h)),
            pl.BlockSpec((1, tb, dv), lambda b, h, i: (b, i, ki + h)),
            pl.BlockSpec((1, tb, dv), lambda b, h, i: (b, i, kz + h)),
            pl.BlockSpec((1, dk), lambda b, h, i: (0, h)),
            pl.BlockSpec((1, dv), lambda b, h, i: (0, 0)),
        ],
        out_specs=pl.BlockSpec((1, tb, dv), lambda b, h, i: (b, i, h)),
        out_shape=jax.ShapeDtypeStruct((bsz, t, heads * dv), BF16),
        scratch_shapes=[pltpu.VMEM((dk, dv), F32)],
        compiler_params=pltpu.CompilerParams(
            dimension_semantics=("parallel", "parallel", "arbitrary"),
            vmem_limit_bytes=VMEM_LIMIT),
        name="hgrn2_mixer",
    )(p, p, p, p, lower_bound.reshape(1, fdim), norm_w.reshape(1, dv))


def _rwkv_kernel(r_ref, k_ref, v_ref, z_ref, wa_ref,
                 rh_ref, kh_ref, vh_ref, zh_ref, wah_ref,
                 mu_r, mu_k, mu_v, mu_z, mu_wa,
                 w0_ref, a0_ref, kk_ref, ka_ref, rk_ref,
                 wl_ref, al_ref, lnw_ref, lnb_ref,
                 o_ref,
                 s_scr, buf_r, buf_k, buf_v, buf_z, buf_wa,
                 wr_scr, u0_scr, arb_scr, y0_scr, bdec_scr, sadd_scr, smul_scr,
                 bonus_scr, gate_scr, *, pipe, dh):
    l = pl.program_id(0)
    _, _, i_local = pipe.decode(pipe.local_block(l))
    tb = r_ref.shape[1]
    lora = wl_ref.shape[0]
    heads = LANE // dh
    n = heads * CHUNK
    slot = lax.rem(l, 2)
    prev = 1 - slot
    rows = tb // GROUPS

    @pl.when(l == 0)
    def _():
        for ref in (s_scr, wr_scr, u0_scr, arb_scr, y0_scr, bdec_scr, sadd_scr, smul_scr,
                    bonus_scr, gate_scr):
            ref[...] = jnp.zeros_like(ref)

    for x_ref, halo_ref, b in ((r_ref, rh_ref, buf_r), (k_ref, kh_ref, buf_k),
                               (v_ref, vh_ref, buf_v), (z_ref, zh_ref, buf_z),
                               (wa_ref, wah_ref, buf_wa)):
        _fill_shift_buffer(b, x_ref, halo_ref, i_local > 0)

    def shifted(b, mu_ref, lo, after):
        cur = b[SUBLANE + lo:SUBLANE + lo + rows, :]
        before = b[SUBLANE - 1 + lo:SUBLANE - 1 + lo + rows, :]
        return cur + (before - cur) * _after(mu_ref[...], after)

    first_matmul = {}

    def local_work(group):
        lo = group * rows
        chunks = [slice(c * CHUNK, (c + 1) * CHUNK) for c in range(rows // CHUNK)]
        after = first_matmul.get(group - 1)
        r = shifted(buf_r, mu_r, lo, after)
        k = shifted(buf_k, mu_k, lo, after)
        v = shifted(buf_v, mu_v, lo, after)
        z = shifted(buf_z, mu_z, lo, after)
        wa = shifted(buf_wa, mu_wa, lo, after)
        wd = wa[:, :lora]
        ad = wa[:, lora:]
        w_log = -_softplus(-(w0_ref[...] + _mm(jnp.tanh(wd), wl_ref[...]))) - 0.5
        log_w = -jnp.exp(w_log)
        a = _sigmoid(a0_ref[...] + _mm(ad, al_ref[...]))
        k2 = k * (1.0 + (a - 1.0) * ka_ref[...])
        kkx = k * kk_ref[...]
        kk = kkx * lax.rsqrt(_head_sum(kkx * kkx, dh) + RWKV_L2_EPS)
        bonus_scr[slot, lo:lo + rows, :] = _head_sum(r * k2 * rk_ref[...], dh) * v
        gate_scr[slot, lo:lo + rows, :] = _silu(z)
        kka = kk * a
        gcum = _chunk_cumsum(log_w)
        e_neg = jnp.exp(-gcum)
        a_t = -kk * jnp.exp(gcum - log_w)
        r_t = r * jnp.exp(gcum)
        b_t = kka * e_neg
        k_t = k2 * e_neg
        ri, ci = _iotas(n)
        same_head = _same_block(ri, ci, CHUNK)
        strict = same_head & (ri > ci)
        incl = same_head & (ri >= ci)
        a_bd = [_head_rows(a_t[sl], dh) for sl in chunks]
        r_bd = [_head_rows(r_t[sl], dh) for sl in chunks]
        v_bd = [_head_rows(v[sl], dh) for sl in chunks]
        first = group * len(chunks)
        k_dec = []
        for c, sl in enumerate(chunks):
            g_last = gcum[sl.stop - 1:sl.stop, :]
            e_last = jnp.exp(g_last - gcum[sl])
            wr_scr[slot, first + c, n:, :] = r_bd[c].astype(BF16)
            bdec_scr[slot, first + c] = _head_rows(kka[sl] * e_last, dh).astype(BF16)
            smul_scr[slot, first + c] = jnp.broadcast_to(
                _row_to_col(jnp.exp(g_last), LANE), (LANE, LANE))
            k_dec.append(_head_rows(k2[sl] * e_last, dh))
        yield
        pm = [_mm_nt(jnp.concatenate([a_bd[c], r_bd[c]], axis=0),
                     jnp.concatenate([b_t[sl]] * heads + [k_t[sl]] * heads, axis=0))
              for c, sl in enumerate(chunks)]
        first_matmul[group] = pm[0][0:1, 0:1]
        for c in range(len(chunks)):
            sadd_scr[slot, first + c] = _mm_tn(k_dec[c], v_bd[c])
        yield
        for c, m in enumerate(pm):
            arb_scr[slot, first + c] = jnp.where(incl, m[n:, :n], 0.0).astype(BF16)
        a_krk = [jnp.concatenate([jnp.where(strict, m[:n, n:], 0.0),
                                  jnp.where(incl, m[n:, n:], 0.0)], axis=0) for m in pm]
        av = [_mm(x, vb) for x, vb in zip(a_krk, v_bd)]
        for c, x in enumerate(av):
            y0_scr[slot, first + c] = x[n:]
        tinv = yield from _unit_lower_inverses(
            [jnp.where(strict, -m[:n, :n], 0.0) for m in pm], CHUNK)
        tw = [_mm(ti, jnp.concatenate([ab, x[:n].astype(BF16)], axis=1))
              for ti, ab, x in zip(tinv, a_bd, av)]
        yield
        for c in range(len(chunks)):
            wr_scr[slot, first + c, :n, :] = tw[c][:, :LANE].astype(BF16)
            u0_scr[slot, first + c] = tw[c][:, LANE:]
        yield

    def state_chain():
        s = jnp.where(lax.rem(l - 1, pipe.nt) == 0, 0.0, s_scr[...])
        for c in range(tb // CHUNK):
            sl = slice(c * CHUNK, (c + 1) * CHUNK)
            rs = _mm(wr_scr[prev, c], s)
            yield
            u = rs[:n] + u0_scr[prev, c]
            y_bd = rs[n:] + _mm(arb_scr[prev, c], u) + y0_scr[prev, c]
            s = s * smul_scr[prev, c] + _mm_tn(bdec_scr[prev, c], u) + sadd_scr[prev, c]
            y = y_bd[:CHUNK]
            for hh in range(1, heads):
                y = y + y_bd[hh * CHUNK:(hh + 1) * CHUNK]
            mean = _head_sum(y, dh) * (1.0 / dh)
            yc = y - mean
            var = _head_sum(yc * yc, dh) * (1.0 / dh)
            yn = yc * lax.rsqrt(var + dh * RWKV_GN_EPS_PER_DIM)
            out = (yn * lnw_ref[...] + lnb_ref[...] + bonus_scr[prev, sl, :]) * gate_scr[prev, sl, :]
            o_ref[0, sl, :] = out.astype(o_ref.dtype)
            yield
        s_scr[...] = s

    _alternate((state_chain(), 0), *[(local_work(g), 2 * g) for g in range(GROUPS)])


def _rwkv_mixer(p, mu, w0, w_lora_up, a0, a_lora_up, k_k, k_a, r_k, ln_w, ln_b, dh):
    bsz, t, _ = p.shape
    branch = w0.shape[0]
    lora = w_lora_up.shape[0]
    assert 2 * lora == LANE and LANE % dh == 0
    tb = min(TIME_BLOCK, t)
    nc = tb // CHUNK
    n = (LANE // dh) * CHUNK
    nb = branch // LANE
    pipe = _Pipeline(bsz, nb, t // tb)
    wa_blk = 4 * nb
    in_specs = (
        [pipe.stream((1, tb, LANE), lambda h, j=j: j * nb + h) for j in range(4)]
        + [pipe.stream((1, tb, LANE), lambda h: wa_blk)]
        + [pipe.halo(LANE, lambda h, j=j: j * nb + h, tb) for j in range(4)]
        + [pipe.halo(LANE, lambda h: wa_blk, tb)]
        + [pipe.per_head((1, LANE), lambda h, j=j: j * nb + h) for j in range(4)]
        + [pipe.per_head((1, LANE), lambda h: wa_blk)]
        + [pipe.per_head((1, LANE), lambda h: h)] * 5
        + [pipe.per_head((lora, LANE), lambda h: h)] * 2
        + [pipe.chain_per_head((1, LANE), lambda h: h)] * 2
    )
    row = lambda a: a.reshape(1, -1)
    return pl.pallas_call(
        functools.partial(_rwkv_kernel, pipe=pipe, dh=dh),
        grid=(pipe.steps,),
        in_specs=in_specs,
        out_specs=pipe.chain((1, tb, LANE), lambda h: h),
        out_shape=jax.ShapeDtypeStruct((bsz, t, branch), BF16),
        scratch_shapes=[
            pltpu.VMEM((LANE, LANE), F32),
        ] + [pltpu.VMEM((tb + SUBLANE, LANE), F32)] * 5 + [
            pltpu.VMEM((2, nc, 2 * n, LANE), BF16),
            pltpu.VMEM((2, nc, n, LANE), F32),
            pltpu.VMEM((2, nc, n, n), BF16),
            pltpu.VMEM((2, nc, n, LANE), F32),
            pltpu.VMEM((2, nc, n, LANE), BF16),
            pltpu.VMEM((2, nc, LANE, LANE), F32),
            pltpu.VMEM((2, nc, LANE, LANE), F32),
            pltpu.VMEM((2, tb, LANE), F32),
            pltpu.VMEM((2, tb, LANE), F32),
        ],
        compiler_params=pltpu.CompilerParams(
            dimension_semantics=("arbitrary",), vmem_limit_bytes=VMEM_LIMIT),
        name="rwkv7_mixer",
    )(p, p, p, p, p, p, p, p, p, p,
      row(mu), row(mu), row(mu), row(mu), row(mu),
      row(w0), row(a0), row(k_k), row(k_a), row(r_k),
      w_lora_up, a_lora_up, row(ln_w), row(ln_b))


def kernel(x, c, ada_w, ada_b, norm_w, gdn_w_in, gdn_conv_w, gdn_a_log, gdn_dt_bias, gdn_norm_w, gdn_w_out, hgrn_w_in, hgrn_lb_logits, hgrn_norm_w, hgrn_w_out, rwkv_w_in, rwkv_mu, rwkv_w0, rwkv_w_lora_up, rwkv_a0, rwkv_a_lora_up, rwkv_k_k, rwkv_k_a, rwkv_r_k, rwkv_ln_w, rwkv_ln_b, rwkv_w_out, final_norm_w):
    depth, d = norm_w.shape
    n_mixers = 3
    gdn_heads = gdn_a_log.shape[1]
    gdn_dv = gdn_norm_w.shape[1]
    gdn_dk = (gdn_conv_w.shape[2] - gdn_heads * gdn_dv) // (2 * gdn_heads)
    hgrn_dv = hgrn_norm_w.shape[1]
    hgrn_f = hgrn_lb_logits.shape[1]
    hgrn_heads = (hgrn_w_in.shape[2] - 2 * hgrn_f) // (2 * hgrn_dv)
    hgrn_dk = hgrn_f // hgrn_heads
    rwkv_dh = rwkv_r_k.shape[2]

    x = x.astype(F32)
    mod = _modulations(c, ada_w, ada_b)
    lower_bounds = _lower_bounds(hgrn_lb_logits)
    for i in range(depth):
        shift, scale, gate = mod[i, :, :d], mod[i, :, d:2 * d], mod[i, :, 2 * d:]
        j = i // n_mixers
        kind = i % n_mixers
        if kind == 0:
            p = _in_projection(x, norm_w[i], scale, shift, gdn_w_in[j])
            og = _gdn_mixer(p, gdn_conv_w[j], gdn_a_log[j], gdn_dt_bias[j], gdn_norm_w[j],
                            gdn_heads, gdn_dk, gdn_dv)
            w_out = gdn_w_out[j]
        elif kind == 1:
            p = _in_projection(x, norm_w[i], scale, shift, hgrn_w_in[j])
            og = _hgrn_mixer(p, lower_bounds[i], hgrn_norm_w[j], hgrn_heads, hgrn_dk, hgrn_dv)
            w_out = hgrn_w_out[j]
        else:
            p = _in_projection(x, norm_w[i], scale, shift, rwkv_w_in[j])
            og = _rwkv_mixer(p, rwkv_mu[j], rwkv_w0[j], rwkv_w_lora_up[j], rwkv_a0[j],
                             rwkv_a_lora_up[j], rwkv_k_k[j], rwkv_k_a[j], rwkv_r_k[j],
                             rwkv_ln_w[j], rwkv_ln_b[j], rwkv_dh)
            w_out = rwkv_w_out[j]
        x = _out_projection(og, w_out, x, gate, final_norm_w, final=(i == depth - 1))
    return x
```

```python
import functools

import jax
import jax.numpy as jnp
from jax import lax
from jax.experimental import pallas as pl
from jax.experimental.pallas import tpu as pltpu

F32 = jnp.float32
BF16 = jnp.bfloat16
HIGHEST = lax.Precision.HIGHEST

NORM_EPS = 1e-6
L2_EPS = 1e-6
RWKV_L2_EPS = 1e-12
RWKV_GN_EPS_PER_DIM = 1e-5

LANE = 128
SUBLANE = 8
HALO_ROWS = 16
MXU_COLS = 256
ROW_TILE = 1024
MAX_COL_TILE = 3072
CHUNK = 64
SUB = 8
TIME_BLOCK = 512
GROUPS = 2
VMEM_LIMIT = 48 * 1024 * 1024


def _dot(a, b):
    return jnp.dot(a, b, preferred_element_type=F32, precision=HIGHEST)


def _mm(a, b):
    return jnp.dot(a.astype(BF16), b.astype(BF16), preferred_element_type=F32)


def _mm_nt(a, b):
    return lax.dot_general(a.astype(BF16), b.astype(BF16), (((1,), (1,)), ((), ())),
                           preferred_element_type=F32)


def _mm_tn(a, b):
    return lax.dot_general(a.astype(BF16), b.astype(BF16), (((0,), (0,)), ((), ())),
                           preferred_element_type=F32)


def _sigmoid(x):
    return 1.0 / (1.0 + jnp.exp(-x))


def _silu(x):
    return x * _sigmoid(x)


def _softplus(x):
    return jnp.maximum(x, 0.0) + jnp.log(1.0 + jnp.exp(-jnp.abs(x)))


def _iotas(n):
    ri = lax.broadcasted_iota(jnp.int32, (n, n), 0)
    ci = lax.broadcasted_iota(jnp.int32, (n, n), 1)
    return ri, ci


def _same_block(ri, ci, b):
    shift = b.bit_length() - 1
    assert 1 << shift == b
    return (ri >> shift) == (ci >> shift)


def _unit_lower_inverses(mats, top):
    n = mats[0].shape[0]
    ri, ci = _iotas(n)
    eye = (ri == ci).astype(BF16)
    zero = jnp.zeros((n, n), BF16)
    same = _same_block(ri, ci, SUB)
    mats = [a.astype(BF16) for a in mats]
    nb = [jnp.where(same, a, zero) for a in mats]
    nb2 = [_mm(x, x).astype(BF16) for x in nb]
    yield
    nb4 = [_mm(x, x).astype(BF16) for x in nb2]
    inv = [_mm(eye - x, eye + x2).astype(BF16) for x, x2 in zip(nb, nb2)]
    yield
    inv = [_mm(x, eye + x4).astype(BF16) for x, x4 in zip(inv, nb4)]
    yield
    b = SUB
    while b < top:
        coupled = _same_block(ri, ci, 2 * b) & jnp.logical_not(_same_block(ri, ci, b))
        off = [jnp.where(coupled, a, zero) for a in mats]
        tmp = [_mm(x, o).astype(BF16) for x, o in zip(inv, off)]
        yield
        inv = [x - _mm(t, x).astype(BF16) for x, t in zip(inv, tmp)]
        yield
        b *= 2
    return inv


def _alternate(*gens_and_delays):
    live = [[g, d] for g, d in gens_and_delays]
    while live:
        for entry in list(live):
            if entry[1] > 0:
                entry[1] -= 1
                continue
            try:
                next(entry[0])
            except StopIteration:
                live.remove(entry)


def _after(x, anchor):
    if anchor is None:
        return x
    return x + jnp.minimum(jnp.abs(anchor), 0.0)


def _chunk_cumsum(x):
    rows, w = x.shape
    ri, ci = _iotas(rows)
    tril = jnp.where(_same_block(ri, ci, CHUNK) & (ri >= ci), 1.0, 0.0).astype(BF16)
    hi = x.astype(BF16)
    rest = x - hi.astype(F32)
    mid = rest.astype(BF16)
    lo = (rest - mid.astype(F32)).astype(BF16)
    parts = jnp.dot(tril, jnp.concatenate([hi, mid, lo], axis=1), preferred_element_type=F32)
    return parts[:, :w] + parts[:, w:2 * w] + parts[:, 2 * w:]


def _head_sum(x, dh):
    lane = lax.broadcasted_iota(jnp.int32, x.shape, 1)
    out = jnp.zeros_like(x)
    for g in range(x.shape[1] // dh):
        in_group = (lane >= g * dh) & (lane < (g + 1) * dh)
        total = jnp.sum(jnp.where(in_group, x, 0.0), axis=-1, keepdims=True)
        out = jnp.where(in_group, total, out)
    return out


def _head_rows(x, dh):
    x = x.astype(BF16)
    lane = lax.broadcasted_iota(jnp.int32, x.shape, 1)
    parts = []
    for g in range(x.shape[1] // dh):
        in_group = (lane >= g * dh) & (lane < (g + 1) * dh)
        parts.append(jnp.where(in_group, x, jnp.zeros_like(x)))
    return jnp.concatenate(parts, axis=0)


def _row_to_col(row, n):
    ri, ci = _iotas(n)
    return jnp.sum(jnp.where(ri == ci, row, 0.0), axis=1, keepdims=True)


def _mod_kernel(c_ref, w_ref, b_ref, o_ref):
    c = c_ref[...]
    o_ref[0] = _dot(_silu(c), w_ref[0]) + b_ref[0]


def _modulations(c, ada_w, ada_b):
    depth, d, d3 = ada_w.shape
    bsz = c.shape[0]
    rows = -(-bsz // SUBLANE) * SUBLANE
    c_pad = jnp.zeros((rows, d), F32).at[:bsz].set(c.astype(F32))
    out = pl.pallas_call(
        _mod_kernel,
        grid=(depth, d3 // d),
        in_specs=[
            pl.BlockSpec((rows, d), lambda i, j: (0, 0)),
            pl.BlockSpec((1, d, d), lambda i, j: (i, 0, j)),
            pl.BlockSpec((1, 1, d), lambda i, j: (i, 0, j)),
        ],
        out_specs=pl.BlockSpec((1, rows, d), lambda i, j: (i, 0, j)),
        out_shape=jax.ShapeDtypeStruct((depth, rows, d3), F32),
        compiler_params=pltpu.CompilerParams(
            dimension_semantics=("parallel", "parallel"), vmem_limit_bytes=VMEM_LIMIT),
        name="adaln_mod",
    )(c_pad, ada_w, ada_b.reshape(depth, 1, d3))
    return out[:, :bsz]


def _lb_kernel(x_ref, o_ref):
    x = x_ref[...]
    depth = x.shape[0]
    m = jnp.max(x, axis=0, keepdims=True)
    e = jnp.exp(x - m)
    p = e / jnp.sum(e, axis=0, keepdims=True)
    acc = jnp.zeros_like(p[0:1])
    for i in range(depth):
        acc = acc + p[i:i + 1]
        o_ref[i:i + 1, :] = acc - p[0:1]


def _lower_bounds(logits):
    return pl.pallas_call(
        _lb_kernel,
        out_shape=jax.ShapeDtypeStruct(logits.shape, F32),
        name="hgrn_lower_bounds",
    )(logits.astype(F32))


def _inproj_kernel(x_ref, nw_ref, sc_ref, sh_ref, w_ref, o_ref, h_scr):
    @pl.when(pl.program_id(2) == 0)
    def _():
        x = x_ref[0]
        ms = jnp.mean(x * x, axis=-1, keepdims=True)
        h = x * lax.rsqrt(ms + NORM_EPS) * nw_ref[...]
        h = h * (1.0 + sc_ref[0]) + sh_ref[0]
        h_scr[...] = h.astype(BF16)

    o_ref[0] = jnp.dot(h_scr[...], w_ref[...], preferred_element_type=F32).astype(o_ref.dtype)


def _col_tile(n):
    best = MXU_COLS
    for k in range(1, n // MXU_COLS + 1):
        if n % (k * MXU_COLS) == 0 and k * MXU_COLS <= MAX_COL_TILE:
            best = k * MXU_COLS
    return best


def _in_projection(x, norm_w, scale, shift, w_in):
    bsz, t, d = x.shape
    n = w_in.shape[1]
    n_pad = -(-n // MXU_COLS) * MXU_COLS
    w = w_in.astype(BF16)
    if n_pad != n:
        w = jnp.pad(w, ((0, 0), (0, n_pad - n)))
    tm = min(ROW_TILE, t)
    tn = _col_tile(n_pad)
    return pl.pallas_call(
        _inproj_kernel,
        grid=(bsz, t // tm, n_pad // tn),
        in_specs=[
            pl.BlockSpec((1, tm, d), lambda b, i, j: (b, i, 0)),
            pl.BlockSpec((1, d), lambda b, i, j: (0, 0)),
            pl.BlockSpec((1, 1, d), lambda b, i, j: (b, 0, 0)),
            pl.BlockSpec((1, 1, d), lambda b, i, j: (b, 0, 0)),
            pl.BlockSpec((d, tn), lambda b, i, j: (0, j)),
        ],
        out_specs=pl.BlockSpec((1, tm, tn), lambda b, i, j: (b, i, j)),
        out_shape=jax.ShapeDtypeStruct((bsz, t, n_pad), BF16),
        scratch_shapes=[pltpu.VMEM((tm, d), BF16)],
        compiler_params=pltpu.CompilerParams(
            dimension_semantics=("parallel", "parallel", "arbitrary"),
            vmem_limit_bytes=VMEM_LIMIT),
        name="norm_inproj",
    )(x, norm_w.reshape(1, d), scale.reshape(bsz, 1, d), shift.reshape(bsz, 1, d), w)


def _outproj_kernel(og_ref, w_ref, x_ref, g_ref, fw_ref, o_ref, *, final):
    y = jnp.dot(og_ref[0], w_ref[...], preferred_element_type=F32)
    xn = x_ref[0] + g_ref[0] * y
    if final:
        ms = jnp.mean(xn * xn, axis=-1, keepdims=True)
        xn = xn * lax.rsqrt(ms + NORM_EPS) * fw_ref[...]
    o_ref[0] = xn


def _out_projection(og, w_out, x, gate, final_w, final):
    bsz, t, d = x.shape
    kdim = og.shape[-1]
    tm = min(ROW_TILE, t)
    return pl.pallas_call(
        functools.partial(_outproj_kernel, final=final),
        grid=(bsz, t // tm),
        in_specs=[
            pl.BlockSpec((1, tm, kdim), lambda b, i: (b, i, 0)),
            pl.BlockSpec((kdim, d), lambda b, i: (0, 0)),
            pl.BlockSpec((1, tm, d), lambda b, i: (b, i, 0)),
            pl.BlockSpec((1, 1, d), lambda b, i: (b, 0, 0)),
            pl.BlockSpec((1, d), lambda b, i: (0, 0)),
        ],
        out_specs=pl.BlockSpec((1, tm, d), lambda b, i: (b, i, 0)),
        out_shape=jax.ShapeDtypeStruct((bsz, t, d), F32),
        compiler_params=pltpu.CompilerParams(
            dimension_semantics=("parallel", "parallel"), vmem_limit_bytes=VMEM_LIMIT),
        name="outproj_residual",
    )(og, w_out.astype(BF16), x, gate.reshape(bsz, 1, d), final_w.reshape(1, d))


def _fill_shift_buffer(buf, x_ref, halo_ref, has_prev):
    halo = halo_ref[0, HALO_ROWS - SUBLANE:, :].astype(F32)
    buf[0:SUBLANE, :] = jnp.where(has_prev, halo, 0.0)
    buf[SUBLANE:, :] = x_ref[0].astype(F32)


class _Pipeline:
    def __init__(self, bsz, heads, nt):
        self.bsz, self.heads, self.nt = bsz, heads, nt
        self.blocks = bsz * heads * nt
        self.steps = self.blocks + 1

    def decode(self, blk):
        return blk // (self.heads * self.nt), (blk // self.nt) % self.heads, blk % self.nt

    def local_block(self, l):
        return jnp.minimum(l, self.blocks - 1)

    def chain_block(self, l):
        return jnp.maximum(l - 1, 0)

    def stream(self, shape, col_fn):
        def index(l):
            b, h, i = self.decode(self.local_block(l))
            return b, i, col_fn(h)
        return pl.BlockSpec(shape, index)

    def halo(self, width, col_fn, tb):
        per = tb // HALO_ROWS

        def index(l):
            b, h, i = self.decode(self.local_block(l))
            return b, jnp.maximum(i * per - 1, 0), col_fn(h)
        return pl.BlockSpec((1, HALO_ROWS, width), index)

    def per_head(self, shape, col_fn):
        def index(l):
            _, h, _ = self.decode(self.local_block(l))
            return 0, col_fn(h)
        return pl.BlockSpec(shape, index)

    def chain(self, shape, col_fn):
        def index(l):
            b, h, i = self.decode(self.chain_block(l))
            return b, i, col_fn(h)
        return pl.BlockSpec(shape, index)

    def chain_per_head(self, shape, col_fn):
        def index(l):
            _, h, _ = self.decode(self.chain_block(l))
            return 0, col_fn(h)
        return pl.BlockSpec(shape, index)


def _gdn_kernel(alog_ref, dtb_ref,
                q_ref, k_ref, v_ref, z_ref, ba_ref,
                qh_ref, kh_ref, vh_ref,
                cq_ref, ck_ref, cv_ref, nw_ref,
                o_ref,
                s_scr, bq, bk, bv, u_scr, wq_scr, attn_scr, kdec_scr, egl_scr, *, pipe, dk):
    l = pl.program_id(0)
    _, h, i_local = pipe.decode(pipe.local_block(l))
    heads = pipe.heads
    tb = q_ref.shape[1]
    dv = v_ref.shape[2]
    taps = cq_ref.shape[0]
    slot = lax.rem(l, 2)
    prev = 1 - slot
    rows = tb // GROUPS

    @pl.when(l == 0)
    def _():
        for ref in (s_scr, u_scr, wq_scr, attn_scr, kdec_scr, egl_scr):
            ref[...] = jnp.zeros_like(ref)

    _fill_shift_buffer(bq, q_ref, qh_ref, i_local > 0)
    _fill_shift_buffer(bk, k_ref, kh_ref, i_local > 0)
    _fill_shift_buffer(bv, v_ref, vh_ref, i_local > 0)

    def conv_silu(buf, cw_ref, lo, after):
        cw = _after(cw_ref[...], after)
        base = SUBLANE - (taps - 1) + lo
        acc = buf[base:base + rows, :] * cw[0:1, :]
        for j in range(1, taps):
            acc = acc + buf[base + j:base + j + rows, :] * cw[j:j + 1, :]
        return _silu(acc)

    first_matmul = {}

    def local_work(group):
        lo = group * rows
        chunks = [slice(c * CHUNK, (c + 1) * CHUNK) for c in range(rows // CHUNK)]
        after = first_matmul.get(group - 1)
        q = conv_silu(bq, cq_ref, lo, after)
        k = conv_silu(bk, ck_ref, lo, after)
        v = conv_silu(bv, cv_ref, lo, after)
        q = q * lax.rsqrt(jnp.sum(q * q, axis=-1, keepdims=True) + L2_EPS) * (dk ** -0.5)
        k = k * lax.rsqrt(jnp.sum(k * k, axis=-1, keepdims=True) + L2_EPS)
        ba = ba_ref[0, lo:lo + rows, :].astype(F32)
        lane = lax.broadcasted_iota(jnp.int32, ba.shape, 1)
        bcol = jnp.sum(jnp.where(lane == h, ba, 0.0), axis=1, keepdims=True)
        acol = jnp.sum(jnp.where(lane == h + heads, ba, 0.0), axis=1, keepdims=True)
        beta = _sigmoid(bcol)
        a_scale = jnp.exp(jnp.full((1, 1), alog_ref[h], F32))
        g = -a_scale * _softplus(acol + dtb_ref[h])
        ri, ci = _iotas(CHUNK)
        incl = ri >= ci
        strict = ri > ci
        gc_col, decay = [], []
        for sl in chunks:
            gcol = g[sl]
            g_row = jnp.sum(jnp.where(ri == ci, gcol, 0.0), axis=0, keepdims=True)
            gc_col.append(jnp.sum(jnp.where(incl, g_row, 0.0), axis=1, keepdims=True))
            gc_row = jnp.sum(jnp.where(ri <= ci, gcol, 0.0), axis=0, keepdims=True)
            decay.append(jnp.where(incl, jnp.exp(gc_col[-1] - gc_row), 0.0))
        kb = [k[sl] * beta[sl] for sl in chunks]
        yield
        kq = [_mm_nt(jnp.concatenate([x, q[sl]], axis=0), k[sl]) for x, sl in zip(kb, chunks)]
        first_matmul[group] = kq[0][0:1, 0:1]
        yield
        tinv = yield from _unit_lower_inverses(
            [jnp.where(strict, m[:CHUNK] * d, 0.0) for m, d in zip(kq, decay)], CHUNK)
        e_gc = [jnp.exp(x) for x in gc_col]
        uw = [_mm(ti, jnp.concatenate([v[sl] * beta[sl], x * e], axis=1))
              for ti, sl, x, e in zip(tinv, chunks, kb, e_gc)]
        yield
        for c, sl in enumerate(chunks):
            gc = group * len(chunks) + c
            g_last = gc_col[c][CHUNK - 1:CHUNK, :]
            u_scr[slot, gc] = uw[c][:, :dv]
            wq_scr[slot, gc] = jnp.concatenate([uw[c][:, dv:], q[sl] * e_gc[c]], axis=0).astype(BF16)
            attn_scr[slot, gc] = (kq[c][CHUNK:] * decay[c]).astype(BF16)
            kdec_scr[slot, gc] = (k[sl] * jnp.exp(g_last - gc_col[c])).astype(BF16)
            egl_scr[slot, gc] = jnp.broadcast_to(jnp.exp(g_last), egl_scr.shape[2:])

    def state_chain():
        nw = nw_ref[...]
        s = jnp.where(lax.rem(l - 1, pipe.nt) == 0, 0.0, s_scr[...])
        for c in range(tb // CHUNK):
            sl = slice(c * CHUNK, (c + 1) * CHUNK)
            ws = _mm(wq_scr[prev, c], s)
            yield
            v_new = u_scr[prev, c] - ws[:CHUNK]
            o = ws[CHUNK:] + _mm(attn_scr[prev, c], v_new)
            s = s * egl_scr[prev, c][0:1, 0:1] + _mm_tn(kdec_scr[prev, c], v_new)
            ms = jnp.mean(o * o, axis=-1, keepdims=True)
            on = o * lax.rsqrt(ms + NORM_EPS) * nw
            o_ref[0, sl, :] = (on * _silu(z_ref[0, sl, :].astype(F32))).astype(o_ref.dtype)
            yield
        s_scr[...] = s

    _alternate((state_chain(), 0), *[(local_work(g), 2 * g) for g in range(GROUPS)])


def _gdn_mixer(p, conv_w, a_log, dt_bias, norm_w, heads, dk, dv):
    bsz, t, _ = p.shape
    tb = min(TIME_BLOCK, t)
    nc = tb // CHUNK
    pipe = _Pipeline(bsz, heads, t // tb)
    qk = heads * dk
    kq = qk // dk
    kv = 2 * qk // dv
    kz = (2 * qk + heads * dv) // dv
    kba = (2 * qk + 2 * heads * dv) // LANE
    taps = conv_w.shape[0]
    smem = pl.BlockSpec(memory_space=pltpu.SMEM)
    in_specs = [
        smem, smem,
        pipe.stream((1, tb, dk), lambda h: h),
        pipe.stream((1, tb, dk), lambda h: kq + h),
        pipe.stream((1, tb, dv), lambda h: kv + h),
        pipe.chain((1, tb, dv), lambda h: kz + h),
        pipe.stream((1, tb, LANE), lambda h: kba),
        pipe.halo(dk, lambda h: h, tb),
        pipe.halo(dk, lambda h: kq + h, tb),
        pipe.halo(dv, lambda h: kv + h, tb),
        pipe.per_head((taps, dk), lambda h: h),
        pipe.per_head((taps, dk), lambda h: kq + h),
        pipe.per_head((taps, dv), lambda h: kv + h),
        pl.BlockSpec((1, dv), lambda l: (0, 0)),
    ]
    return pl.pallas_call(
        functools.partial(_gdn_kernel, pipe=pipe, dk=dk),
        grid=(pipe.steps,),
        in_specs=in_specs,
        out_specs=pipe.chain((1, tb, dv), lambda h: h),
        out_shape=jax.ShapeDtypeStruct((bsz, t, heads * dv), BF16),
        scratch_shapes=[
            pltpu.VMEM((dk, dv), F32),
            pltpu.VMEM((tb + SUBLANE, dk), F32),
            pltpu.VMEM((tb + SUBLANE, dk), F32),
            pltpu.VMEM((tb + SUBLANE, dv), F32),
            pltpu.VMEM((2, nc, CHUNK, dv), F32),
            pltpu.VMEM((2, nc, 2 * CHUNK, dk), BF16),
            pltpu.VMEM((2, nc, CHUNK, CHUNK), BF16),
            pltpu.VMEM((2, nc, CHUNK, dk), BF16),
            pltpu.VMEM((2, nc, SUBLANE, LANE), F32),
        ],
        compiler_params=pltpu.CompilerParams(
            dimension_semantics=("arbitrary",), vmem_limit_bytes=VMEM_LIMIT),
        name="gdn_mixer",
    )(a_log.astype(F32), dt_bias.astype(F32), p, p, p, p, p, p, p, p,
      conv_w, conv_w, conv_w, norm_w.reshape(1, dv))


def _hgrn_kernel(q_ref, f_ref, i_ref, z_ref, lb_ref, nw_ref, o_ref, s_scr):
    t = pl.program_id(2)
    tb = q_ref.shape[1]
    dk = q_ref.shape[2]

    @pl.when(t == 0)
    def _():
        s_scr[...] = jnp.zeros_like(s_scr)

    lb = lb_ref[...]
    q = _silu(q_ref[0].astype(F32))
    f = f_ref[0].astype(F32)
    a1 = jnp.log(lb)
    a2 = jnp.log(1.0 - lb) - _softplus(-f)
    log_f = jnp.maximum(a1, a2) + jnp.log(1.0 + jnp.exp(-jnp.abs(a1 - a2)))
    k = (1.0 - lb) * _sigmoid(-f)
    v = i_ref[0].astype(F32)
    z = z_ref[0].astype(F32)
    nw = nw_ref[...]

    ri, ci = _iotas(CHUNK)
    rows = lax.broadcasted_iota(jnp.int32, (CHUNK, 1), 0)
    gcum_all = _chunk_cumsum(log_f)
    chunks = [slice(c * CHUNK, (c + 1) * CHUNK) for c in range(tb // CHUNK)]
    scores, s_mul, q_dec = [], [], []
    for sl in chunks:
        qc, kc = q[sl], k[sl]
        gcum = gcum_all[sl]
        g_last = gcum[CHUNK - 1:CHUNK, :]
        blocks = []
        for i in range(CHUNK // SUB):
            lo, hi = i * SUB, (i + 1) * SUB
            ref = gcum[lo - 1:lo, :] if i > 0 else jnp.zeros_like(g_last)
            qs = qc[lo:hi] * jnp.exp(gcum[lo:hi] - ref)
            ks = jnp.where(rows < hi, kc * jnp.exp(ref - gcum), 0.0)
            blocks.append(_mm_nt(qs, ks))
        scores.append(blocks)
        s_mul.append(_row_to_col(jnp.exp(g_last), dk))
        q_dec.append(qc * jnp.exp(gcum))
    s_add = [_mm_tn(k[sl] * jnp.exp(gcum_all[sl.stop - 1:sl.stop, :] - gcum_all[sl]), v[sl])
             for sl in chunks]
    o_intra = [_mm(jnp.where(ri >= ci, jnp.concatenate(blocks, axis=0), 0.0), v[sl])
               for blocks, sl in zip(scores, chunks)]

    s = s_scr[...]
    for c, sl in enumerate(chunks):
        o = _mm(q_dec[c], s) + o_intra[c]
        s = s * s_mul[c] + s_add[c]
        ms = jnp.mean(o * o, axis=-1, keepdims=True)
        on = o * lax.rsqrt(ms + NORM_EPS) * nw
        o_ref[0, sl, :] = (on * _silu(z[sl])).astype(o_ref.dtype)
    s_scr[...] = s


def _hgrn_mixer(p, lower_bound, norm_w, heads, dk, dv):
    bsz, t, _ = p.shape
    tb = min(TIME_BLOCK, t)
    fdim = heads * dk
    kf = fdim // dk
    ki = 2 * fdim // dv
    kz = (2 * fdim + heads * dv) // dv
    return pl.pallas_call(
        _hgrn_kernel,
        grid=(bsz, heads, t // tb),
        in_specs=[
            pl.BlockSpec((1, tb, dk), lambda b, h, i: (b, i, h)),
            pl.BlockSpec((1, tb, dk), lambda b, h, i: (b, i, kf + h)),
            pl.BlockSpec((1, tb, dv), lambda b, h, i: (b, i, ki + h)),
            pl.BlockSpec((1, tb, dv), lambda b, h, i: (b, i, kz + h)),
            pl.BlockSpec((1, dk), lambda b, h, i: (0, h)),
            pl.BlockSpec((1, dv), lambda b, h, i: (0, 0)),
        ],
        out_specs=pl.BlockSpec((1, tb, dv), lambda b, h, i: (b, i, h)),
        out_shape=jax.ShapeDtypeStruct((bsz, t, heads * dv), BF16),
        scratch_shapes=[pltpu.VMEM((dk, dv), F32)],
        compiler_params=pltpu.CompilerParams(
            dimension_semantics=("parallel", "parallel", "arbitrary"),
            vmem_limit_bytes=VMEM_LIMIT),
        name="hgrn2_mixer",
    )(p, p, p, p, lower_bound.reshape(1, fdim), norm_w.reshape(1, dv))


def _rwkv_kernel(r_ref, k_ref, v_ref, z_ref, wa_ref,
                 rh_ref, kh_ref, vh_ref, zh_ref, wah_ref,
                 mu_r, mu_k, mu_v, mu_z, mu_wa,
                 w0_ref, a0_ref, kk_ref, ka_ref, rk_ref,
                 wl_ref, al_ref, lnw_ref, lnb_ref,
                 o_ref,
                 s_scr, buf_r, buf_k, buf_v, buf_z, buf_wa,
                 wr_scr, u0_scr, arb_scr, y0_scr, bdec_scr, sadd_scr, smul_scr,
                 bonus_scr, gate_scr, *, pipe, dh):
    l = pl.program_id(0)
    _, _, i_local = pipe.decode(pipe.local_block(l))
    tb = r_ref.shape[1]
    lora = wl_ref.shape[0]
    heads = LANE // dh
    n = heads * CHUNK
    slot = lax.rem(l, 2)
    prev = 1 - slot
    rows = tb // GROUPS

    @pl.when(l == 0)
    def _():
        for ref in (s_scr, wr_scr, u0_scr, arb_scr, y0_scr, bdec_scr, sadd_scr, smul_scr,
                    bonus_scr, gate_scr):
            ref[...] = jnp.zeros_like(ref)

    for x_ref, halo_ref, b in ((r_ref, rh_ref, buf_r), (k_ref, kh_ref, buf_k),
                               (v_ref, vh_ref, buf_v), (z_ref, zh_ref, buf_z),
                               (wa_ref, wah_ref, buf_wa)):
        _fill_shift_buffer(b, x_ref, halo_ref, i_local > 0)

    def shifted(b, mu_ref, lo, after):
        cur = b[SUBLANE + lo:SUBLANE + lo + rows, :]
        before = b[SUBLANE - 1 + lo:SUBLANE - 1 + lo + rows, :]
        return cur + (before - cur) * _after(mu_ref[...], after)

    first_matmul = {}

    def local_work(group):
        lo = group * rows
        chunks = [slice(c * CHUNK, (c + 1) * CHUNK) for c in range(rows // CHUNK)]
        after = first_matmul.get(group - 1)
        r = shifted(buf_r, mu_r, lo, after)
        k = shifted(buf_k, mu_k, lo, after)
        v = shifted(buf_v, mu_v, lo, after)
        z = shifted(buf_z, mu_z, lo, after)
        wa = shifted(buf_wa, mu_wa, lo, after)
        wd = wa[:, :lora]
        ad = wa[:, lora:]
        w_log = -_softplus(-(w0_ref[...] + _mm(jnp.tanh(wd), wl_ref[...]))) - 0.5
        log_w = -jnp.exp(w_log)
        a = _sigmoid(a0_ref[...] + _mm(ad, al_ref[...]))
        k2 = k * (1.0 + (a - 1.0) * ka_ref[...])
        kkx = k * kk_ref[...]
        kk = kkx * lax.rsqrt(_head_sum(kkx * kkx, dh) + RWKV_L2_EPS)
        bonus_scr[slot, lo:lo + rows, :] = _head_sum(r * k2 * rk_ref[...], dh) * v
        gate_scr[slot, lo:lo + rows, :] = _silu(z)
        kka = kk * a
        gcum = _chunk_cumsum(log_w)
        e_neg = jnp.exp(-gcum)
        a_t = -kk * jnp.exp(gcum - log_w)
        r_t = r * jnp.exp(gcum)
        b_t = kka * e_neg
        k_t = k2 * e_neg
        ri, ci = _iotas(n)
        same_head = _same_block(ri, ci, CHUNK)
        strict = same_head & (ri > ci)
        incl = same_head & (ri >= ci)
        a_bd = [_head_rows(a_t[sl], dh) for sl in chunks]
        r_bd = [_head_rows(r_t[sl], dh) for sl in chunks]
        v_bd = [_head_rows(v[sl], dh) for sl in chunks]
        first = group * len(chunks)
        k_dec = []
        for c, sl in enumerate(chunks):
            g_last = gcum[sl.stop - 1:sl.stop, :]
            e_last = jnp.exp(g_last - gcum[sl])
            wr_scr[slot, first + c, n:, :] = r_bd[c].astype(BF16)
            bdec_scr[slot, first + c] = _head_rows(kka[sl] * e_last, dh).astype(BF16)
            smul_scr[slot, first + c] = jnp.broadcast_to(
                _row_to_col(jnp.exp(g_last), LANE), (LANE, LANE))
            k_dec.append(_head_rows(k2[sl] * e_last, dh))
        yield
        pm = [_mm_nt(jnp.concatenate([a_bd[c], r_bd[c]], axis=0),
                     jnp.concatenate([b_t[sl]] * heads + [k_t[sl]] * heads, axis=0))
              for c, sl in enumerate(chunks)]
        first_matmul[group] = pm[0][0:1, 0:1]
        for c in range(len(chunks)):
            sadd_scr[slot, first + c] = _mm_tn(k_dec[c], v_bd[c])
        yield
        for c, m in enumerate(pm):
            arb_scr[slot, first + c] = jnp.where(incl, m[n:, :n], 0.0).astype(BF16)
        a_krk = [jnp.concatenate([jnp.where(strict, m[:n, n:], 0.0),
                                  jnp.where(incl, m[n:, n:], 0.0)], axis=0) for m in pm]
        av = [_mm(x, vb) for x, vb in zip(a_krk, v_bd)]
        for c, x in enumerate(av):
            y0_scr[slot, first + c] = x[n:]
        av_top = [x[:n].astype(BF16) for x in av]
        tinv = yield from _unit_lower_inverses(
            [jnp.where(strict, -m[:n, :n], 0.0) for m in pm], CHUNK)
        tw = [_mm(ti, jnp.concatenate([ab, x], axis=1)) for ti, ab, x in zip(tinv, a_bd, av_top)]
        yield
        for c in range(len(chunks)):
            wr_scr[slot, first + c, :n, :] = tw[c][:, :LANE].astype(BF16)
            u0_scr[slot, first + c] = tw[c][:, LANE:]
        yield

    def state_chain():
        s = jnp.where(lax.rem(l - 1, pipe.nt) == 0, 0.0, s_scr[...])
        for c in range(tb // CHUNK):
            sl = slice(c * CHUNK, (c + 1) * CHUNK)
            rs = _mm(wr_scr[prev, c], s)
            yield
            u = rs[:n] + u0_scr[prev, c]
            y_bd = rs[n:] + _mm(arb_scr[prev, c], u) + y0_scr[prev, c]
            s = s * smul_scr[prev, c] + _mm_tn(bdec_scr[prev, c], u) + sadd_scr[prev, c]
            y = y_bd[:CHUNK]
            for hh in range(1, heads):
                y = y + y_bd[hh * CHUNK:(hh + 1) * CHUNK]
            mean = _head_sum(y, dh) * (1.0 / dh)
            yc = y - mean
            var = _head_sum(yc * yc, dh) * (1.0 / dh)
            yn = yc * lax.rsqrt(var + dh * RWKV_GN_EPS_PER_DIM)
            out = (yn * lnw_ref[...] + lnb_ref[...] + bonus_scr[prev, sl, :]) * gate_scr[prev, sl, :]
            o_ref[0, sl, :] = out.astype(o_ref.dtype)
            yield
        s_scr[...] = s

    _alternate((state_chain(), 0), *[(local_work(g), 2 * g) for g in range(GROUPS)])


def _rwkv_mixer(p, mu, w0, w_lora_up, a0, a_lora_up, k_k, k_a, r_k, ln_w, ln_b, dh):
    bsz, t, _ = p.shape
    branch = w0.shape[0]
    lora = w_lora_up.shape[0]
    assert 2 * lora == LANE and LANE % dh == 0
    tb = min(TIME_BLOCK, t)
    nc = tb // CHUNK
    n = (LANE // dh) * CHUNK
    nb = branch // LANE
    pipe = _Pipeline(bsz, nb, t // tb)
    wa_blk = 4 * nb
    in_specs = (
        [pipe.stream((1, tb, LANE), lambda h, j=j: j * nb + h) for j in range(4)]
        + [pipe.stream((1, tb, LANE), lambda h: wa_blk)]
        + [pipe.halo(LANE, lambda h, j=j: j * nb + h, tb) for j in range(4)]
        + [pipe.halo(LANE, lambda h: wa_blk, tb)]
        + [pipe.per_head((1, LANE), lambda h, j=j: j * nb + h) for j in range(4)]
        + [pipe.per_head((1, LANE), lambda h: wa_blk)]
        + [pipe.per_head((1, LANE), lambda h: h)] * 5
        + [pipe.per_head((lora, LANE), lambda h: h)] * 2
        + [pipe.chain_per_head((1, LANE), lambda h: h)] * 2
    )
    row = lambda a: a.reshape(1, -1)
    return pl.pallas_call(
        functools.partial(_rwkv_kernel, pipe=pipe, dh=dh),
        grid=(pipe.steps,),
        in_specs=in_specs,
        out_specs=pipe.chain((1, tb, LANE), lambda h: h),
        out_shape=jax.ShapeDtypeStruct((bsz, t, branch), BF16),
        scratch_shapes=[
            pltpu.VMEM((LANE, LANE), F32),
        ] + [pltpu.VMEM((tb + SUBLANE, LANE), F32)] * 5 + [
            pltpu.VMEM((2, nc, 2 * n, LANE), BF16),
            pltpu.VMEM((2, nc, n, LANE), F32),
            pltpu.VMEM((2, nc, n, n), BF16),
            pltpu.VMEM((2, nc, n, LANE), F32),
            pltpu.VMEM((2, nc, n, LANE), BF16),
            pltpu.VMEM((2, nc, LANE, LANE), F32),
            pltpu.VMEM((2, nc, LANE, LANE), F32),
            pltpu.VMEM((2, tb, LANE), F32),
            pltpu.VMEM((2, tb, LANE), F32),
        ],
        compiler_params=pltpu.CompilerParams(
            dimension_semantics=("arbitrary",), vmem_limit_bytes=VMEM_LIMIT),
        name="rwkv7_mixer",
    )(p, p, p, p, p, p, p, p, p, p,
      row(mu), row(mu), row(mu), row(mu), row(mu),
      row(w0), row(a0), row(k_k), row(k_a), row(r_k),
      w_lora_up, a_lora_up, row(ln_w), row(ln_b))


def kernel(x, c, ada_w, ada_b, norm_w, gdn_w_in, gdn_conv_w, gdn_a_log, gdn_dt_bias, gdn_norm_w, gdn_w_out, hgrn_w_in, hgrn_lb_logits, hgrn_norm_w, hgrn_w_out, rwkv_w_in, rwkv_mu, rwkv_w0, rwkv_w_lora_up, rwkv_a0, rwkv_a_lora_up, rwkv_k_k, rwkv_k_a, rwkv_r_k, rwkv_ln_w, rwkv_ln_b, rwkv_w_out, final_norm_w):
    depth, d = norm_w.shape
    n_mixers = 3
    gdn_heads = gdn_a_log.shape[1]
    gdn_dv = gdn_norm_w.shape[1]
    gdn_dk = (gdn_conv_w.shape[2] - gdn_heads * gdn_dv) // (2 * gdn_heads)
    hgrn_dv = hgrn_norm_w.shape[1]
    hgrn_f = hgrn_lb_logits.shape[1]
    hgrn_heads = (hgrn_w_in.shape[2] - 2 * hgrn_f) // (2 * hgrn_dv)
    hgrn_dk = hgrn_f // hgrn_heads
    rwkv_dh = rwkv_r_k.shape[2]

    x = x.astype(F32)
    mod = _modulations(c, ada_w, ada_b)
    lower_bounds = _lower_bounds(hgrn_lb_logits)
    for i in range(depth):
        shift, scale, gate = mod[i, :, :d], mod[i, :, d:2 * d], mod[i, :, 2 * d:]
        j = i // n_mixers
        kind = i % n_mixers
        if kind == 0:
            p = _in_projection(x, norm_w[i], scale, shift, gdn_w_in[j])
            og = _gdn_mixer(p, gdn_conv_w[j], gdn_a_log[j], gdn_dt_bias[j], gdn_norm_w[j],
                            gdn_heads, gdn_dk, gdn_dv)
            w_out = gdn_w_out[j]
        elif kind == 1:
            p = _in_projection(x, norm_w[i], scale, shift, hgrn_w_in[j])
            og = _hgrn_mixer(p, lower_bounds[i], hgrn_norm_w[j], hgrn_heads, hgrn_dk, hgrn_dv)
            w_out = hgrn_w_out[j]
        else:
            p = _in_projection(x, norm_w[i], scale, shift, rwkv_w_in[j])
            og = _rwkv_mixer(p, rwkv_mu[j], rwkv_w0[j], rwkv_w_lora_up[j], rwkv_a0[j],
                             rwkv_a_lora_up[j], rwkv_k_k[j], rwkv_k_a[j], rwkv_r_k[j],
                             rwkv_ln_w[j], rwkv_ln_b[j], rwkv_dh)
            w_out = rwkv_w_out[j]
        x = _out_projection(og, w_out, x, gate, final_norm_w, final=(i == depth - 1))
    return x
```

```python
import functools

import jax
import jax.numpy as jnp
from jax import lax
from jax.experimental import pallas as pl
from jax.experimental.pallas import tpu as pltpu

F32 = jnp.float32
BF16 = jnp.bfloat16
HIGHEST = lax.Precision.HIGHEST

NORM_EPS = 1e-6
L2_EPS = 1e-6
RWKV_L2_EPS = 1e-12
RWKV_GN_EPS_PER_DIM = 1e-5

LANE = 128
SUBLANE = 8
HALO_ROWS = 16
MXU_COLS = 256
ROW_TILE = 1024
MAX_COL_TILE = 3072
CHUNK = 64
SUB = 8
TIME_BLOCK = 512
GROUPS = 2
VMEM_LIMIT = 48 * 1024 * 1024


def _dot(a, b):
    return jnp.dot(a, b, preferred_element_type=F32, precision=HIGHEST)


def _mm(a, b):
    return jnp.dot(a.astype(BF16), b.astype(BF16), preferred_element_type=F32)


def _mm_nt(a, b):
    return lax.dot_general(a.astype(BF16), b.astype(BF16), (((1,), (1,)), ((), ())),
                           preferred_element_type=F32)


def _mm_tn(a, b):
    return lax.dot_general(a.astype(BF16), b.astype(BF16), (((0,), (0,)), ((), ())),
                           preferred_element_type=F32)


def _sigmoid(x):
    return 1.0 / (1.0 + jnp.exp(-x))


def _silu(x):
    return x * _sigmoid(x)


def _softplus(x):
    return jnp.maximum(x, 0.0) + jnp.log(1.0 + jnp.exp(-jnp.abs(x)))


def _iotas(n):
    ri = lax.broadcasted_iota(jnp.int32, (n, n), 0)
    ci = lax.broadcasted_iota(jnp.int32, (n, n), 1)
    return ri, ci


def _same_block(ri, ci, b):
    shift = b.bit_length() - 1
    assert 1 << shift == b
    return (ri >> shift) == (ci >> shift)


def _unit_lower_inverses(mats, top):
    n = mats[0].shape[0]
    ri, ci = _iotas(n)
    eye = (ri == ci).astype(BF16)
    zero = jnp.zeros((n, n), BF16)
    same = _same_block(ri, ci, SUB)
    mats = [a.astype(BF16) for a in mats]
    nb = [jnp.where(same, a, zero) for a in mats]
    nb2 = [_mm(x, x).astype(BF16) for x in nb]
    yield
    nb4 = [_mm(x, x).astype(BF16) for x in nb2]
    inv = [_mm(eye - x, eye + x2).astype(BF16) for x, x2 in zip(nb, nb2)]
    yield
    inv = [_mm(x, eye + x4).astype(BF16) for x, x4 in zip(inv, nb4)]
    yield
    b = SUB
    while b < top:
        coupled = _same_block(ri, ci, 2 * b) & jnp.logical_not(_same_block(ri, ci, b))
        off = [jnp.where(coupled, a, zero) for a in mats]
        tmp = [_mm(x, o).astype(BF16) for x, o in zip(inv, off)]
        yield
        inv = [x - _mm(t, x).astype(BF16) for x, t in zip(inv, tmp)]
        yield
        b *= 2
    return inv


def _alternate(*gens_and_delays):
    live = [[g, d] for g, d in gens_and_delays]
    while live:
        for entry in list(live):
            if entry[1] > 0:
                entry[1] -= 1
                continue
            try:
                next(entry[0])
            except StopIteration:
                live.remove(entry)


def _after(x, anchor):
    if anchor is None:
        return x
    return x + jnp.minimum(jnp.abs(anchor), 0.0)


def _chunk_cumsum(x):
    rows, w = x.shape
    ri, ci = _iotas(rows)
    tril = jnp.where(_same_block(ri, ci, CHUNK) & (ri >= ci), 1.0, 0.0).astype(BF16)
    hi = x.astype(BF16)
    rest = x - hi.astype(F32)
    mid = rest.astype(BF16)
    lo = (rest - mid.astype(F32)).astype(BF16)
    parts = jnp.dot(tril, jnp.concatenate([hi, mid, lo], axis=1), preferred_element_type=F32)
    return parts[:, :w] + parts[:, w:2 * w] + parts[:, 2 * w:]


def _head_sum(x, dh):
    lane = lax.broadcasted_iota(jnp.int32, x.shape, 1)
    out = jnp.zeros_like(x)
    for g in range(x.shape[1] // dh):
        in_group = (lane >= g * dh) & (lane < (g + 1) * dh)
        total = jnp.sum(jnp.where(in_group, x, 0.0), axis=-1, keepdims=True)
        out = jnp.where(in_group, total, out)
    return out


def _head_rows(x, dh):
    x = x.astype(BF16)
    lane = lax.broadcasted_iota(jnp.int32, x.shape, 1)
    parts = []
    for g in range(x.shape[1] // dh):
        in_group = (lane >= g * dh) & (lane < (g + 1) * dh)
        parts.append(jnp.where(in_group, x, jnp.zeros_like(x)))
    return jnp.concatenate(parts, axis=0)


def _row_to_col(row, n):
    ri, ci = _iotas(n)
    return jnp.sum(jnp.where(ri == ci, row, 0.0), axis=1, keepdims=True)


def _mod_kernel(c_ref, w_ref, b_ref, o_ref):
    c = c_ref[...]
    o_ref[0] = _dot(_silu(c), w_ref[0]) + b_ref[0]


def _modulations(c, ada_w, ada_b):
    depth, d, d3 = ada_w.shape
    bsz = c.shape[0]
    rows = -(-bsz // SUBLANE) * SUBLANE
    c_pad = jnp.zeros((rows, d), F32).at[:bsz].set(c.astype(F32))
    out = pl.pallas_call(
        _mod_kernel,
        grid=(depth, d3 // d),
        in_specs=[
            pl.BlockSpec((rows, d), lambda i, j: (0, 0)),
            pl.BlockSpec((1, d, d), lambda i, j: (i, 0, j)),
            pl.BlockSpec((1, 1, d), lambda i, j: (i, 0, j)),
        ],
        out_specs=pl.BlockSpec((1, rows, d), lambda i, j: (i, 0, j)),
        out_shape=jax.ShapeDtypeStruct((depth, rows, d3), F32),
        compiler_params=pltpu.CompilerParams(
            dimension_semantics=("parallel", "parallel"), vmem_limit_bytes=VMEM_LIMIT),
        name="adaln_mod",
    )(c_pad, ada_w, ada_b.reshape(depth, 1, d3))
    return out[:, :bsz]


def _lb_kernel(x_ref, o_ref):
    x = x_ref[...]
    depth = x.shape[0]
    m = jnp.max(x, axis=0, keepdims=True)
    e = jnp.exp(x - m)
    p = e / jnp.sum(e, axis=0, keepdims=True)
    acc = jnp.zeros_like(p[0:1])
    for i in range(depth):
        acc = acc + p[i:i + 1]
        o_ref[i:i + 1, :] = acc - p[0:1]


def _lower_bounds(logits):
    return pl.pallas_call(
        _lb_kernel,
        out_shape=jax.ShapeDtypeStruct(logits.shape, F32),
        name="hgrn_lower_bounds",
    )(logits.astype(F32))


def _inproj_kernel(x_ref, nw_ref, sc_ref, sh_ref, w_ref, o_ref, h_scr):
    @pl.when(pl.program_id(2) == 0)
    def _():
        x = x_ref[0]
        ms = jnp.mean(x * x, axis=-1, keepdims=True)
        h = x * lax.rsqrt(ms + NORM_EPS) * nw_ref[...]
        h = h * (1.0 + sc_ref[0]) + sh_ref[0]
        h_scr[...] = h.astype(BF16)

    o_ref[0] = jnp.dot(h_scr[...], w_ref[0], preferred_element_type=F32).astype(o_ref.dtype)


def _col_tile(n):
    best = MXU_COLS
    for k in range(1, n // MXU_COLS + 1):
        if n % (k * MXU_COLS) == 0 and k * MXU_COLS <= MAX_COL_TILE:
            best = k * MXU_COLS
    return best


def _projection_weights(w_in):
    n = w_in.shape[2]
    n_pad = -(-n // MXU_COLS) * MXU_COLS
    return jnp.pad(w_in.astype(BF16), ((0, 0), (0, 0), (0, n_pad - n)))


def _in_projection(x, norm_w, scale, shift, w, layer):
    bsz, t, d = x.shape
    n_pad = w.shape[2]
    tm = min(ROW_TILE, t)
    tn = _col_tile(n_pad)
    return pl.pallas_call(
        _inproj_kernel,
        grid=(bsz, t // tm, n_pad // tn),
        in_specs=[
            pl.BlockSpec((1, tm, d), lambda b, i, j: (b, i, 0)),
            pl.BlockSpec((1, d), lambda b, i, j: (0, 0)),
            pl.BlockSpec((1, 1, d), lambda b, i, j: (b, 0, 0)),
            pl.BlockSpec((1, 1, d), lambda b, i, j: (b, 0, 0)),
            pl.BlockSpec((1, d, tn), lambda b, i, j: (layer, 0, j)),
        ],
        out_specs=pl.BlockSpec((1, tm, tn), lambda b, i, j: (b, i, j)),
        out_shape=jax.ShapeDtypeStruct((bsz, t, n_pad), BF16),
        scratch_shapes=[pltpu.VMEM((tm, d), BF16)],
        compiler_params=pltpu.CompilerParams(
            dimension_semantics=("parallel", "parallel", "arbitrary"),
            vmem_limit_bytes=VMEM_LIMIT),
        name="norm_inproj",
    )(x, norm_w.reshape(1, d), scale.reshape(bsz, 1, d), shift.reshape(bsz, 1, d), w)


def _outproj_kernel(og_ref, w_ref, x_ref, g_ref, fw_ref, o_ref, *, final):
    y = jnp.dot(og_ref[0], w_ref[...], preferred_element_type=F32)
    xn = x_ref[0] + g_ref[0] * y
    if final:
        ms = jnp.mean(xn * xn, axis=-1, keepdims=True)
        xn = xn * lax.rsqrt(ms + NORM_EPS) * fw_ref[...]
    o_ref[0] = xn


def _out_projection(og, w_out, x, gate, final_w, final):
    bsz, t, d = x.shape
    kdim = og.shape[-1]
    tm = min(ROW_TILE, t)
    return pl.pallas_call(
        functools.partial(_outproj_kernel, final=final),
        grid=(bsz, t // tm),
        in_specs=[
            pl.BlockSpec((1, tm, kdim), lambda b, i: (b, i, 0)),
            pl.BlockSpec((kdim, d), lambda b, i: (0, 0)),
            pl.BlockSpec((1, tm, d), lambda b, i: (b, i, 0)),
            pl.BlockSpec((1, 1, d), lambda b, i: (b, 0, 0)),
            pl.BlockSpec((1, d), lambda b, i: (0, 0)),
        ],
        out_specs=pl.BlockSpec((1, tm, d), lambda b, i: (b, i, 0)),
        out_shape=jax.ShapeDtypeStruct((bsz, t, d), F32),
        compiler_params=pltpu.CompilerParams(
            dimension_semantics=("parallel", "parallel"), vmem_limit_bytes=VMEM_LIMIT),
        name="outproj_residual",
    )(og, w_out.astype(BF16), x, gate.reshape(bsz, 1, d), final_w.reshape(1, d))


def _fill_shift_buffer(buf, x_ref, halo_ref, has_prev):
    halo = halo_ref[0, HALO_ROWS - SUBLANE:, :].astype(F32)
    buf[0:SUBLANE, :] = jnp.where(has_prev, halo, 0.0)
    buf[SUBLANE:, :] = x_ref[0].astype(F32)


class _Pipeline:
    def __init__(self, bsz, heads, nt):
        self.bsz, self.heads, self.nt = bsz, heads, nt
        self.blocks = bsz * heads * nt
        self.steps = self.blocks + 1

    def decode(self, blk):
        return blk // (self.heads * self.nt), (blk // self.nt) % self.heads, blk % self.nt

    def local_block(self, l):
        return jnp.minimum(l, self.blocks - 1)

    def chain_block(self, l):
        return jnp.maximum(l - 1, 0)

    def stream(self, shape, col_fn):
        def index(l):
            b, h, i = self.decode(self.local_block(l))
            return b, i, col_fn(h)
        return pl.BlockSpec(shape, index)

    def halo(self, width, col_fn, tb):
        per = tb // HALO_ROWS

        def index(l):
            b, h, i = self.decode(self.local_block(l))
            return b, jnp.maximum(i * per - 1, 0), col_fn(h)
        return pl.BlockSpec((1, HALO_ROWS, width), index)

    def per_head(self, shape, col_fn):
        def index(l):
            _, h, _ = self.decode(self.local_block(l))
            return 0, col_fn(h)
        return pl.BlockSpec(shape, index)

    def chain(self, shape, col_fn):
        def index(l):
            b, h, i = self.decode(self.chain_block(l))
            return b, i, col_fn(h)
        return pl.BlockSpec(shape, index)

    def chain_per_head(self, shape, col_fn):
        def index(l):
            _, h, _ = self.decode(self.chain_block(l))
            return 0, col_fn(h)
        return pl.BlockSpec(shape, index)


def _gdn_kernel(alog_ref, dtb_ref,
                q_ref, k_ref, v_ref, z_ref, ba_ref,
                qh_ref, kh_ref, vh_ref,
                cq_ref, ck_ref, cv_ref, nw_ref,
                o_ref,
                s_scr, bq, bk, bv, u_scr, wq_scr, attn_scr, kdec_scr, egl_scr, *, pipe, dk):
    l = pl.program_id(0)
    _, h, i_local = pipe.decode(pipe.local_block(l))
    heads = pipe.heads
    tb = q_ref.shape[1]
    dv = v_ref.shape[2]
    taps = cq_ref.shape[0]
    slot = lax.rem(l, 2)
    prev = 1 - slot
    rows = tb // GROUPS

    @pl.when(l == 0)
    def _():
        for ref in (s_scr, u_scr, wq_scr, attn_scr, kdec_scr, egl_scr):
            ref[...] = jnp.zeros_like(ref)

    _fill_shift_buffer(bq, q_ref, qh_ref, i_local > 0)
    _fill_shift_buffer(bk, k_ref, kh_ref, i_local > 0)
    _fill_shift_buffer(bv, v_ref, vh_ref, i_local > 0)

    def conv_silu(buf, cw_ref, lo, after):
        cw = _after(cw_ref[...], after)
        base = SUBLANE - (taps - 1) + lo
        acc = buf[base:base + rows, :] * cw[0:1, :]
        for j in range(1, taps):
            acc = acc + buf[base + j:base + j + rows, :] * cw[j:j + 1, :]
        return _silu(acc)

    first_matmul = {}

    def local_work(group):
        lo = group * rows
        chunks = [slice(c * CHUNK, (c + 1) * CHUNK) for c in range(rows // CHUNK)]
        after = first_matmul.get(group - 1)
        q = conv_silu(bq, cq_ref, lo, after)
        k = conv_silu(bk, ck_ref, lo, after)
        v = conv_silu(bv, cv_ref, lo, after)
        q = q * lax.rsqrt(jnp.sum(q * q, axis=-1, keepdims=True) + L2_EPS) * (dk ** -0.5)
        k = k * lax.rsqrt(jnp.sum(k * k, axis=-1, keepdims=True) + L2_EPS)
        ba = ba_ref[0, lo:lo + rows, :].astype(F32)
        lane = lax.broadcasted_iota(jnp.int32, ba.shape, 1)
        bcol = jnp.sum(jnp.where(lane == h, ba, 0.0), axis=1, keepdims=True)
        acol = jnp.sum(jnp.where(lane == h + heads, ba, 0.0), axis=1, keepdims=True)
        beta = _sigmoid(bcol)
        a_scale = jnp.exp(jnp.full((1, 1), alog_ref[h], F32))
        g = -a_scale * _softplus(acol + dtb_ref[h])
        ri, ci = _iotas(CHUNK)
        incl = ri >= ci
        strict = ri > ci
        gc_col, decay = [], []
        for sl in chunks:
            gcol = g[sl]
            g_row = jnp.sum(jnp.where(ri == ci, gcol, 0.0), axis=0, keepdims=True)
            gc_col.append(jnp.sum(jnp.where(incl, g_row, 0.0), axis=1, keepdims=True))
            gc_row = jnp.sum(jnp.where(ri <= ci, gcol, 0.0), axis=0, keepdims=True)
            decay.append(jnp.where(incl, jnp.exp(gc_col[-1] - gc_row), 0.0))
        kb = [k[sl] * beta[sl] for sl in chunks]
        yield
        kq = [_mm_nt(jnp.concatenate([x, q[sl]], axis=0), k[sl]) for x, sl in zip(kb, chunks)]
        first_matmul[group] = kq[0][0:1, 0:1]
        yield
        tinv = yield from _unit_lower_inverses(
            [jnp.where(strict, m[:CHUNK] * d, 0.0) for m, d in zip(kq, decay)], CHUNK)
        e_gc = [jnp.exp(x) for x in gc_col]
        uw = [_mm(ti, jnp.concatenate([v[sl] * beta[sl], x * e], axis=1))
              for ti, sl, x, e in zip(tinv, chunks, kb, e_gc)]
        yield
        for c, sl in enumerate(chunks):
            gc = group * len(chunks) + c
            g_last = gc_col[c][CHUNK - 1:CHUNK, :]
            u_scr[slot, gc] = uw[c][:, :dv]
            wq_scr[slot, gc] = jnp.concatenate([uw[c][:, dv:], q[sl] * e_gc[c]], axis=0).astype(BF16)
            attn_scr[slot, gc] = (kq[c][CHUNK:] * decay[c]).astype(BF16)
            kdec_scr[slot, gc] = (k[sl] * jnp.exp(g_last - gc_col[c])).astype(BF16)
            egl_scr[slot, gc] = jnp.broadcast_to(jnp.exp(g_last), egl_scr.shape[2:])

    def state_chain():
        nw = nw_ref[...]
        s = jnp.where(lax.rem(l - 1, pipe.nt) == 0, 0.0, s_scr[...])
        for c in range(tb // CHUNK):
            sl = slice(c * CHUNK, (c + 1) * CHUNK)
            ws = _mm(wq_scr[prev, c], s)
            yield
            v_new = u_scr[prev, c] - ws[:CHUNK]
            o = ws[CHUNK:] + _mm(attn_scr[prev, c], v_new)
            s = s * egl_scr[prev, c][0:1, 0:1] + _mm_tn(kdec_scr[prev, c], v_new)
            ms = jnp.mean(o * o, axis=-1, keepdims=True)
            on = o * lax.rsqrt(ms + NORM_EPS) * nw
            o_ref[0, sl, :] = (on * _silu(z_ref[0, sl, :].astype(F32))).astype(o_ref.dtype)
            yield
        s_scr[...] = s

    _alternate((state_chain(), 0), *[(local_work(g), 2 * g) for g in range(GROUPS)])


def _gdn_mixer(p, conv_w, a_log, dt_bias, norm_w, heads, dk, dv):
    bsz, t, _ = p.shape
    tb = min(TIME_BLOCK, t)
    nc = tb // CHUNK
    pipe = _Pipeline(bsz, heads, t // tb)
    qk = heads * dk
    kq = qk // dk
    kv = 2 * qk // dv
    kz = (2 * qk + heads * dv) // dv
    kba = (2 * qk + 2 * heads * dv) // LANE
    taps = conv_w.shape[0]
    smem = pl.BlockSpec(memory_space=pltpu.SMEM)
    in_specs = [
        smem, smem,
        pipe.stream((1, tb, dk), lambda h: h),
        pipe.stream((1, tb, dk), lambda h: kq + h),
        pipe.stream((1, tb, dv), lambda h: kv + h),
        pipe.chain((1, tb, dv), lambda h: kz + h),
        pipe.stream((1, tb, LANE), lambda h: kba),
        pipe.halo(dk, lambda h: h, tb),
        pipe.halo(dk, lambda h: kq + h, tb),
        pipe.halo(dv, lambda h: kv + h, tb),
        pipe.per_head((taps, dk), lambda h: h),
        pipe.per_head((taps, dk), lambda h: kq + h),
        pipe.per_head((taps, dv), lambda h: kv + h),
        pl.BlockSpec((1, dv), lambda l: (0, 0)),
    ]
    return pl.pallas_call(
        functools.partial(_gdn_kernel, pipe=pipe, dk=dk),
        grid=(pipe.steps,),
        in_specs=in_specs,
        out_specs=pipe.chain((1, tb, dv), lambda h: h),
        out_shape=jax.ShapeDtypeStruct((bsz, t, heads * dv), BF16),
        scratch_shapes=[
            pltpu.VMEM((dk, dv), F32),
            pltpu.VMEM((tb + SUBLANE, dk), F32),
            pltpu.VMEM((tb + SUBLANE, dk), F32),
            pltpu.VMEM((tb + SUBLANE, dv), F32),
            pltpu.VMEM((2, nc, CHUNK, dv), F32),
            pltpu.VMEM((2, nc, 2 * CHUNK, dk), BF16),
            pltpu.VMEM((2, nc, CHUNK, CHUNK), BF16),
            pltpu.VMEM((2, nc, CHUNK, dk), BF16),
            pltpu.VMEM((2, nc, SUBLANE, LANE), F32),
        ],
        compiler_params=pltpu.CompilerParams(
            dimension_semantics=("arbitrary",), vmem_limit_bytes=VMEM_LIMIT),
        name="gdn_mixer",
    )(a_log.astype(F32), dt_bias.astype(F32), p, p, p, p, p, p, p, p,
      conv_w, conv_w, conv_w, norm_w.reshape(1, dv))


def _hgrn_kernel(q_ref, f_ref, i_ref, z_ref, lb_ref, nw_ref, o_ref, s_scr):
    t = pl.program_id(2)
    tb = q_ref.shape[1]
    dk = q_ref.shape[2]

    @pl.when(t == 0)
    def _():
        s_scr[...] = jnp.zeros_like(s_scr)

    lb = lb_ref[...]
    q = _silu(q_ref[0].astype(F32))
    f = f_ref[0].astype(F32)
    a1 = jnp.log(lb)
    a2 = jnp.log(1.0 - lb) - _softplus(-f)
    log_f = jnp.maximum(a1, a2) + jnp.log(1.0 + jnp.exp(-jnp.abs(a1 - a2)))
    k = (1.0 - lb) * _sigmoid(-f)
    v = i_ref[0].astype(F32)
    z = z_ref[0].astype(F32)
    nw = nw_ref[...]

    ri, ci = _iotas(CHUNK)
    rows = lax.broadcasted_iota(jnp.int32, (CHUNK, 1), 0)
    gcum_all = _chunk_cumsum(log_f)
    chunks = [slice(c * CHUNK, (c + 1) * CHUNK) for c in range(tb // CHUNK)]
    scores, s_mul, q_dec = [], [], []
    for sl in chunks:
        qc, kc = q[sl], k[sl]
        gcum = gcum_all[sl]
        g_last = gcum[CHUNK - 1:CHUNK, :]
        blocks = []
        for i in range(CHUNK // SUB):
            lo, hi = i * SUB, (i + 1) * SUB
            ref = gcum[lo - 1:lo, :] if i > 0 else jnp.zeros_like(g_last)
            qs = qc[lo:hi] * jnp.exp(gcum[lo:hi] - ref)
            ks = jnp.where(rows < hi, kc * jnp.exp(ref - gcum), 0.0)
            blocks.append(_mm_nt(qs, ks))
        scores.append(blocks)
        s_mul.append(_row_to_col(jnp.exp(g_last), dk))
        q_dec.append(qc * jnp.exp(gcum))
    s_add = [_mm_tn(k[sl] * jnp.exp(gcum_all[sl.stop - 1:sl.stop, :] - gcum_all[sl]), v[sl])
             for sl in chunks]
    o_intra = [_mm(jnp.where(ri >= ci, jnp.concatenate(blocks, axis=0), 0.0), v[sl])
               for blocks, sl in zip(scores, chunks)]

    s = s_scr[...]
    for c, sl in enumerate(chunks):
        o = _mm(q_dec[c], s) + o_intra[c]
        s = s * s_mul[c] + s_add[c]
        ms = jnp.mean(o * o, axis=-1, keepdims=True)
        on = o * lax.rsqrt(ms + NORM_EPS) * nw
        o_ref[0, sl, :] = (on * _silu(z[sl])).astype(o_ref.dtype)
    s_scr[...] = s


def _hgrn_mixer(p, lower_bound, norm_w, heads, dk, dv):
    bsz, t, _ = p.shape
    tb = min(TIME_BLOCK, t)
    fdim = heads * dk
    kf = fdim // dk
    ki = 2 * fdim // dv
    kz = (2 * fdim + heads * dv) // dv
    return pl.pallas_call(
        _hgrn_kernel,
        grid=(bsz, heads, t // tb),
        in_specs=[
            pl.BlockSpec((1, tb, dk), lambda b, h, i: (b, i, h)),
            pl.BlockSpec((1, tb, dk), lambda b, h, i: (b, i, kf + h)),
            pl.BlockSpec((1, tb, dv), lambda b, h, i: (b, i, ki + h)),
            pl.BlockSpec((1, tb, dv), lambda b, h, i: (b, i, kz + h)),
            pl.BlockSpec((1, dk), lambda b, h, i: (0, h)),
            pl.BlockSpec((1, dv), lambda b, h, i: (0, 0)),
        ],
        out_specs=pl.BlockSpec((1, tb, dv), lambda b, h, i: (b, i, h)),
        out_shape=jax.ShapeDtypeStruct((bsz, t, heads * dv), BF16),
        scratch_shapes=[pltpu.VMEM((dk, dv), F32)],
        compiler_params=pltpu.CompilerParams(
            dimension_semantics=("parallel", "parallel", "arbitrary"),
            vmem_limit_bytes=VMEM_LIMIT),
        name="hgrn2_mixer",
    )(p, p, p, p, lower_bound.reshape(1, fdim), norm_w.reshape(1, dv))


def _rwkv_kernel(r_ref, k_ref, v_ref, z_ref, wa_ref,
                 rh_ref, kh_ref, vh_ref, zh_ref, wah_ref,
                 mu_r, mu_k, mu_v, mu_z, mu_wa,
                 w0_ref, a0_ref, kk_ref, ka_ref, rk_ref,
                 wl_ref, al_ref, lnw_ref, lnb_ref,
                 o_ref,
                 s_scr, buf_r, buf_k, buf_v, buf_z, buf_wa,
                 wr_scr, u0_scr, arb_scr, y0_scr, bdec_scr, sadd_scr, smul_scr,
                 bonus_scr, gate_scr, *, pipe, dh):
    l = pl.program_id(0)
    _, _, i_local = pipe.decode(pipe.local_block(l))
    tb = r_ref.shape[1]
    lora = wl_ref.shape[0]
    heads = LANE // dh
    n = heads * CHUNK
    slot = lax.rem(l, 2)
    prev = 1 - slot
    rows = tb // GROUPS

    @pl.when(l == 0)
    def _():
        for ref in (s_scr, wr_scr, u0_scr, arb_scr, y0_scr, bdec_scr, sadd_scr, smul_scr,
                    bonus_scr, gate_scr):
            ref[...] = jnp.zeros_like(ref)

    for x_ref, halo_ref, b in ((r_ref, rh_ref, buf_r), (k_ref, kh_ref, buf_k),
                               (v_ref, vh_ref, buf_v), (z_ref, zh_ref, buf_z),
                               (wa_ref, wah_ref, buf_wa)):
        _fill_shift_buffer(b, x_ref, halo_ref, i_local > 0)

    def shifted(b, mu_ref, lo, after):
        cur = b[SUBLANE + lo:SUBLANE + lo + rows, :]
        before = b[SUBLANE - 1 + lo:SUBLANE - 1 + lo + rows, :]
        return cur + (before - cur) * _after(mu_ref[...], after)

    first_matmul = {}

    def local_work(group):
        lo = group * rows
        chunks = [slice(c * CHUNK, (c + 1) * CHUNK) for c in range(rows // CHUNK)]
        after = first_matmul.get(group - 1)
        r = shifted(buf_r, mu_r, lo, after)
        k = shifted(buf_k, mu_k, lo, after)
        v = shifted(buf_v, mu_v, lo, after)
        z = shifted(buf_z, mu_z, lo, after)
        wa = shifted(buf_wa, mu_wa, lo, after)
        wd = wa[:, :lora]
        ad = wa[:, lora:]
        w_log = -_softplus(-(w0_ref[...] + _mm(jnp.tanh(wd), wl_ref[...]))) - 0.5
        log_w = -jnp.exp(w_log)
        a = _sigmoid(a0_ref[...] + _mm(ad, al_ref[...]))
        k2 = k * (1.0 + (a - 1.0) * ka_ref[...])
        kkx = k * kk_ref[...]
        kk = kkx * lax.rsqrt(_head_sum(kkx * kkx, dh) + RWKV_L2_EPS)
        bonus_scr[slot, lo:lo + rows, :] = _head_sum(r * k2 * rk_ref[...], dh) * v
        gate_scr[slot, lo:lo + rows, :] = _silu(z)
        kka = kk * a
        gcum = _chunk_cumsum(log_w)
        e_neg = jnp.exp(-gcum)
        a_t = -kk * jnp.exp(gcum - log_w)
        r_t = r * jnp.exp(gcum)
        b_t = kka * e_neg
        k_t = k2 * e_neg
        ri, ci = _iotas(n)
        same_head = _same_block(ri, ci, CHUNK)
        strict = same_head & (ri > ci)
        incl = same_head & (ri >= ci)
        a_bd = [_head_rows(a_t[sl], dh) for sl in chunks]
        r_bd = [_head_rows(r_t[sl], dh) for sl in chunks]
        v_bd = [_head_rows(v[sl], dh) for sl in chunks]
        first = group * len(chunks)
        k_dec = []
        for c, sl in enumerate(chunks):
            g_last = gcum[sl.stop - 1:sl.stop, :]
            e_last = jnp.exp(g_last - gcum[sl])
            wr_scr[slot, first + c, n:, :] = r_bd[c].astype(BF16)
            bdec_scr[slot, first + c] = _head_rows(kka[sl] * e_last, dh).astype(BF16)
            smul_scr[slot, first + c] = jnp.broadcast_to(
                _row_to_col(jnp.exp(g_last), LANE), (LANE, LANE))
            k_dec.append(_head_rows(k2[sl] * e_last, dh))
        yield
        pm = [_mm_nt(jnp.concatenate([a_bd[c], r_bd[c]], axis=0),
                     jnp.concatenate([b_t[sl]] * heads + [k_t[sl]] * heads, axis=0))
              for c, sl in enumerate(chunks)]
        first_matmul[group] = pm[0][0:1, 0:1]
        for c in range(len(chunks)):
            sadd_scr[slot, first + c] = _mm_tn(k_dec[c], v_bd[c])
        yield
        for c, m in enumerate(pm):
            arb_scr[slot, first + c] = jnp.where(incl, m[n:, :n], 0.0).astype(BF16)
        a_krk = [jnp.concatenate([jnp.where(strict, m[:n, n:], 0.0),
                                  jnp.where(incl, m[n:, n:], 0.0)], axis=0) for m in pm]
        av = [_mm(x, vb) for x, vb in zip(a_krk, v_bd)]
        for c, x in enumerate(av):
            y0_scr[slot, first + c] = x[n:]
        av_top = [x[:n].astype(BF16) for x in av]
        tinv = yield from _unit_lower_inverses(
            [jnp.where(strict, -m[:n, :n], 0.0) for m in pm], CHUNK)
        tw = [_mm(ti, jnp.concatenate([ab, x], axis=1)) for ti, ab, x in zip(tinv, a_bd, av_top)]
        yield
        for c in range(len(chunks)):
            wr_scr[slot, first + c, :n, :] = tw[c][:, :LANE].astype(BF16)
            u0_scr[slot, first + c] = tw[c][:, LANE:]
        yield

    def state_chain():
        s = jnp.where(lax.rem(l - 1, pipe.nt) == 0, 0.0, s_scr[...])
        for c in range(tb // CHUNK):
            sl = slice(c * CHUNK, (c + 1) * CHUNK)
            rs = _mm(wr_scr[prev, c], s)
            yield
            u = rs[:n] + u0_scr[prev, c]
            y_bd = rs[n:] + _mm(arb_scr[prev, c], u) + y0_scr[prev, c]
            s = s * smul_scr[prev, c] + _mm_tn(bdec_scr[prev, c], u) + sadd_scr[prev, c]
            y = y_bd[:CHUNK]
            for hh in range(1, heads):
                y = y + y_bd[hh * CHUNK:(hh + 1) * CHUNK]
            mean = _head_sum(y, dh) * (1.0 / dh)
            yc = y - mean
            var = _head_sum(yc * yc, dh) * (1.0 / dh)
            yn = yc * lax.rsqrt(var + dh * RWKV_GN_EPS_PER_DIM)
            out = (yn * lnw_ref[...] + lnb_ref[...] + bonus_scr[prev, sl, :]) * gate_scr[prev, sl, :]
            o_ref[0, sl, :] = out.astype(o_ref.dtype)
            yield
        s_scr[...] = s

    _alternate((state_chain(), 0), *[(local_work(g), 2 * g) for g in range(GROUPS)])


def _rwkv_mixer(p, mu, w0, w_lora_up, a0, a_lora_up, k_k, k_a, r_k, ln_w, ln_b, dh):
    bsz, t, _ = p.shape
    branch = w0.shape[0]
    lora = w_lora_up.shape[0]
    assert 2 * lora == LANE and LANE % dh == 0
    tb = min(TIME_BLOCK, t)
    nc = tb // CHUNK
    n = (LANE // dh) * CHUNK
    nb = branch // LANE
    pipe = _Pipeline(bsz, nb, t // tb)
    wa_blk = 4 * nb
    in_specs = (
        [pipe.stream((1, tb, LANE), lambda h, j=j: j * nb + h) for j in range(4)]
        + [pipe.stream((1, tb, LANE), lambda h: wa_blk)]
        + [pipe.halo(LANE, lambda h, j=j: j * nb + h, tb) for j in range(4)]
        + [pipe.halo(LANE, lambda h: wa_blk, tb)]
        + [pipe.per_head((1, LANE), lambda h, j=j: j * nb + h) for j in range(4)]
        + [pipe.per_head((1, LANE), lambda h: wa_blk)]
        + [pipe.per_head((1, LANE), lambda h: h)] * 5
        + [pipe.per_head((lora, LANE), lambda h: h)] * 2
        + [pipe.chain_per_head((1, LANE), lambda h: h)] * 2
    )
    row = lambda a: a.reshape(1, -1)
    return pl.pallas_call(
        functools.partial(_rwkv_kernel, pipe=pipe, dh=dh),
        grid=(pipe.steps,),
        in_specs=in_specs,
        out_specs=pipe.chain((1, tb, LANE), lambda h: h),
        out_shape=jax.ShapeDtypeStruct((bsz, t, branch), BF16),
        scratch_shapes=[
            pltpu.VMEM((LANE, LANE), F32),
        ] + [pltpu.VMEM((tb + SUBLANE, LANE), F32)] * 5 + [
            pltpu.VMEM((2, nc, 2 * n, LANE), BF16),
            pltpu.VMEM((2, nc, n, LANE), F32),
            pltpu.VMEM((2, nc, n, n), BF16),
            pltpu.VMEM((2, nc, n, LANE), F32),
            pltpu.VMEM((2, nc, n, LANE), BF16),
            pltpu.VMEM((2, nc, LANE, LANE), F32),
            pltpu.VMEM((2, nc, LANE, LANE), F32),
            pltpu.VMEM((2, tb, LANE), F32),
            pltpu.VMEM((2, tb, LANE), F32),
        ],
        compiler_params=pltpu.CompilerParams(
            dimension_semantics=("arbitrary",), vmem_limit_bytes=VMEM_LIMIT),
        name="rwkv7_mixer",
    )(p, p, p, p, p, p, p, p, p, p,
      row(mu), row(mu), row(mu), row(mu), row(mu),
      row(w0), row(a0), row(k_k), row(k_a), row(r_k),
      w_lora_up, a_lora_up, row(ln_w), row(ln_b))


def kernel(x, c, ada_w, ada_b, norm_w, gdn_w_in, gdn_conv_w, gdn_a_log, gdn_dt_bias, gdn_norm_w, gdn_w_out, hgrn_w_in, hgrn_lb_logits, hgrn_norm_w, hgrn_w_out, rwkv_w_in, rwkv_mu, rwkv_w0, rwkv_w_lora_up, rwkv_a0, rwkv_a_lora_up, rwkv_k_k, rwkv_k_a, rwkv_r_k, rwkv_ln_w, rwkv_ln_b, rwkv_w_out, final_norm_w):
    depth, d = norm_w.shape
    n_mixers = 3
    gdn_heads = gdn_a_log.shape[1]
    gdn_dv = gdn_norm_w.shape[1]
    gdn_dk = (gdn_conv_w.shape[2] - gdn_heads * gdn_dv) // (2 * gdn_heads)
    hgrn_dv = hgrn_norm_w.shape[1]
    hgrn_f = hgrn_lb_logits.shape[1]
    hgrn_heads = (hgrn_w_in.shape[2] - 2 * hgrn_f) // (2 * hgrn_dv)
    hgrn_dk = hgrn_f // hgrn_heads
    rwkv_dh = rwkv_r_k.shape[2]

    x = x.astype(F32)
    gdn_w = _projection_weights(gdn_w_in)
    hgrn_w = _projection_weights(hgrn_w_in)
    rwkv_w = _projection_weights(rwkv_w_in)
    mod = _modulations(c, ada_w, ada_b)
    lower_bounds = _lower_bounds(hgrn_lb_logits)
    for i in range(depth):
        shift, scale, gate = mod[i, :, :d], mod[i, :, d:2 * d], mod[i, :, 2 * d:]
        j = i // n_mixers
        kind = i % n_mixers
        if kind == 0:
            p = _in_projection(x, norm_w[i], scale, shift, gdn_w, j)
            og = _gdn_mixer(p, gdn_conv_w[j], gdn_a_log[j], gdn_dt_bias[j], gdn_norm_w[j],
                            gdn_heads, gdn_dk, gdn_dv)
            w_out = gdn_w_out[j]
        elif kind == 1:
            p = _in_projection(x, norm_w[i], scale, shift, hgrn_w, j)
            og = _hgrn_mixer(p, lower_bounds[i], hgrn_norm_w[j], hgrn_heads, hgrn_dk, hgrn_dv)
            w_out = hgrn_w_out[j]
        else:
            p = _in_projection(x, norm_w[i], scale, shift, rwkv_w, j)
            og = _rwkv_mixer(p, rwkv_mu[j], rwkv_w0[j], rwkv_w_lora_up[j], rwkv_a0[j],
                             rwkv_a_lora_up[j], rwkv_k_k[j], rwkv_k_a[j], rwkv_r_k[j],
                             rwkv_ln_w[j], rwkv_ln_b[j], rwkv_dh)
            w_out = rwkv_w_out[j]
        x = _out_projection(og, w_out, x, gate, final_norm_w, final=(i == depth - 1))
    return x
```

```python
import functools

import jax
import jax.numpy as jnp
from jax import lax
from jax.experimental import pallas as pl
from jax.experimental.pallas import tpu as pltpu

F32 = jnp.float32
BF16 = jnp.bfloat16
HIGHEST = lax.Precision.HIGHEST

NORM_EPS = 1e-6
L2_EPS = 1e-6
RWKV_L2_EPS = 1e-12
RWKV_GN_EPS_PER_DIM = 1e-5

LANE = 128
SUBLANE = 8
HALO_ROWS = 16
MXU_COLS = 256
ROW_TILE = 1024
MAX_COL_TILE = 3072
CHUNK = 64
SUB = 8
TIME_BLOCK = 512
GROUPS = 2
RWKV_TIME_BLOCK = 1024
RWKV_GROUPS = 4
VMEM_LIMIT = 48 * 1024 * 1024


def _dot(a, b):
    return jnp.dot(a, b, preferred_element_type=F32, precision=HIGHEST)


def _mm(a, b):
    return jnp.dot(a.astype(BF16), b.astype(BF16), preferred_element_type=F32)


def _mm_nt(a, b):
    return lax.dot_general(a.astype(BF16), b.astype(BF16), (((1,), (1,)), ((), ())),
                           preferred_element_type=F32)


def _mm_tn(a, b):
    return lax.dot_general(a.astype(BF16), b.astype(BF16), (((0,), (0,)), ((), ())),
                           preferred_element_type=F32)


def _sigmoid(x):
    return 1.0 / (1.0 + jnp.exp(-x))


def _silu(x):
    return x * _sigmoid(x)


def _softplus(x):
    return jnp.maximum(x, 0.0) + jnp.log(1.0 + jnp.exp(-jnp.abs(x)))


def _iotas(n):
    ri = lax.broadcasted_iota(jnp.int32, (n, n), 0)
    ci = lax.broadcasted_iota(jnp.int32, (n, n), 1)
    return ri, ci


def _same_block(ri, ci, b):
    shift = b.bit_length() - 1
    assert 1 << shift == b
    return (ri >> shift) == (ci >> shift)


def _unit_lower_inverses(mats, top):
    n = mats[0].shape[0]
    ri, ci = _iotas(n)
    eye = (ri == ci).astype(BF16)
    zero = jnp.zeros((n, n), BF16)
    same = _same_block(ri, ci, SUB)
    mats = [a.astype(BF16) for a in mats]
    nb = [jnp.where(same, a, zero) for a in mats]
    nb2 = [_mm(x, x).astype(BF16) for x in nb]
    yield
    nb4 = [_mm(x, x).astype(BF16) for x in nb2]
    inv = [_mm(eye - x, eye + x2).astype(BF16) for x, x2 in zip(nb, nb2)]
    yield
    inv = [_mm(x, eye + x4).astype(BF16) for x, x4 in zip(inv, nb4)]
    yield
    b = SUB
    while b < top:
        coupled = _same_block(ri, ci, 2 * b) & jnp.logical_not(_same_block(ri, ci, b))
        off = [jnp.where(coupled, a, zero) for a in mats]
        tmp = [_mm(x, o).astype(BF16) for x, o in zip(inv, off)]
        yield
        inv = [x - _mm(t, x).astype(BF16) for x, t in zip(inv, tmp)]
        yield
        b *= 2
    return inv


def _alternate(*gens_and_delays):
    live = [[g, d] for g, d in gens_and_delays]
    while live:
        for entry in list(live):
            if entry[1] > 0:
                entry[1] -= 1
                continue
            try:
                next(entry[0])
            except StopIteration:
                live.remove(entry)


def _after(x, anchor):
    if anchor is None:
        return x
    return x + jnp.minimum(jnp.abs(anchor), 0.0)


def _chunk_cumsum(x):
    rows, w = x.shape
    ri, ci = _iotas(rows)
    tril = jnp.where(_same_block(ri, ci, CHUNK) & (ri >= ci), 1.0, 0.0).astype(BF16)
    hi = x.astype(BF16)
    rest = x - hi.astype(F32)
    mid = rest.astype(BF16)
    lo = (rest - mid.astype(F32)).astype(BF16)
    parts = jnp.dot(tril, jnp.concatenate([hi, mid, lo], axis=1), preferred_element_type=F32)
    return parts[:, :w] + parts[:, w:2 * w] + parts[:, 2 * w:]


def _head_sum(x, dh):
    lane = lax.broadcasted_iota(jnp.int32, x.shape, 1)
    out = jnp.zeros_like(x)
    for g in range(x.shape[1] // dh):
        in_group = (lane >= g * dh) & (lane < (g + 1) * dh)
        total = jnp.sum(jnp.where(in_group, x, 0.0), axis=-1, keepdims=True)
        out = jnp.where(in_group, total, out)
    return out


def _head_rows(x, dh):
    x = x.astype(BF16)
    lane = lax.broadcasted_iota(jnp.int32, x.shape, 1)
    parts = []
    for g in range(x.shape[1] // dh):
        in_group = (lane >= g * dh) & (lane < (g + 1) * dh)
        parts.append(jnp.where(in_group, x, jnp.zeros_like(x)))
    return jnp.concatenate(parts, axis=0)


def _row_to_col(row, n):
    ri, ci = _iotas(n)
    return jnp.sum(jnp.where(ri == ci, row, 0.0), axis=1, keepdims=True)


def _mod_kernel(c_ref, w_ref, b_ref, o_ref):
    c = c_ref[...]
    o_ref[0] = _dot(_silu(c), w_ref[0]) + b_ref[0]


def _modulations(c, ada_w, ada_b):
    depth, d, d3 = ada_w.shape
    bsz = c.shape[0]
    rows = -(-bsz // SUBLANE) * SUBLANE
    c_pad = jnp.zeros((rows, d), F32).at[:bsz].set(c.astype(F32))
    out = pl.pallas_call(
        _mod_kernel,
        grid=(depth, d3 // d),
        in_specs=[
            pl.BlockSpec((rows, d), lambda i, j: (0, 0)),
            pl.BlockSpec((1, d, d), lambda i, j: (i, 0, j)),
            pl.BlockSpec((1, 1, d), lambda i, j: (i, 0, j)),
        ],
        out_specs=pl.BlockSpec((1, rows, d), lambda i, j: (i, 0, j)),
        out_shape=jax.ShapeDtypeStruct((depth, rows, d3), F32),
        compiler_params=pltpu.CompilerParams(
            dimension_semantics=("parallel", "parallel"), vmem_limit_bytes=VMEM_LIMIT),
        name="adaln_mod",
    )(c_pad, ada_w, ada_b.reshape(depth, 1, d3))
    return out[:, :bsz]


def _lb_kernel(x_ref, o_ref):
    x = x_ref[...]
    depth = x.shape[0]
    m = jnp.max(x, axis=0, keepdims=True)
    e = jnp.exp(x - m)
    p = e / jnp.sum(e, axis=0, keepdims=True)
    acc = jnp.zeros_like(p[0:1])
    for i in range(depth):
        acc = acc + p[i:i + 1]
        o_ref[i:i + 1, :] = acc - p[0:1]


def _lower_bounds(logits):
    return pl.pallas_call(
        _lb_kernel,
        out_shape=jax.ShapeDtypeStruct(logits.shape, F32),
        name="hgrn_lower_bounds",
    )(logits.astype(F32))


def _inproj_kernel(x_ref, nw_ref, sc_ref, sh_ref, w_ref, o_ref, h_scr):
    @pl.when(pl.program_id(2) == 0)
    def _():
        x = x_ref[0]
        ms = jnp.mean(x * x, axis=-1, keepdims=True)
        h = x * lax.rsqrt(ms + NORM_EPS) * nw_ref[...]
        h = h * (1.0 + sc_ref[0]) + sh_ref[0]
        h_scr[...] = h.astype(BF16)

    o_ref[0] = jnp.dot(h_scr[...], w_ref[...], preferred_element_type=F32).astype(o_ref.dtype)


def _col_tile(n):
    best = MXU_COLS
    for k in range(1, n // MXU_COLS + 1):
        if n % (k * MXU_COLS) == 0 and k * MXU_COLS <= MAX_COL_TILE:
            best = k * MXU_COLS
    return best


def _in_projection(x, norm_w, scale, shift, w_in):
    bsz, t, d = x.shape
    n = w_in.shape[1]
    n_pad = -(-n // MXU_COLS) * MXU_COLS
    w = w_in.astype(BF16)
    if n_pad != n:
        w = jnp.pad(w, ((0, 0), (0, n_pad - n)))
    tm = min(ROW_TILE, t)
    tn = _col_tile(n_pad)
    return pl.pallas_call(
        _inproj_kernel,
        grid=(bsz, t // tm, n_pad // tn),
        in_specs=[
            pl.BlockSpec((1, tm, d), lambda b, i, j: (b, i, 0)),
            pl.BlockSpec((1, d), lambda b, i, j: (0, 0)),
            pl.BlockSpec((1, 1, d), lambda b, i, j: (b, 0, 0)),
            pl.BlockSpec((1, 1, d), lambda b, i, j: (b, 0, 0)),
            pl.BlockSpec((d, tn), lambda b, i, j: (0, j)),
        ],
        out_specs=pl.BlockSpec((1, tm, tn), lambda b, i, j: (b, i, j)),
        out_shape=jax.ShapeDtypeStruct((bsz, t, n_pad), BF16),
        scratch_shapes=[pltpu.VMEM((tm, d), BF16)],
        compiler_params=pltpu.CompilerParams(
            dimension_semantics=("parallel", "parallel", "arbitrary"),
            vmem_limit_bytes=VMEM_LIMIT),
        name="norm_inproj",
    )(x, norm_w.reshape(1, d), scale.reshape(bsz, 1, d), shift.reshape(bsz, 1, d), w)


def _outproj_kernel(og_ref, w_ref, x_ref, g_ref, fw_ref, o_ref, *, final):
    y = jnp.dot(og_ref[0], w_ref[...], preferred_element_type=F32)
    xn = x_ref[0] + g_ref[0] * y
    if final:
        ms = jnp.mean(xn * xn, axis=-1, keepdims=True)
        xn = xn * lax.rsqrt(ms + NORM_EPS) * fw_ref[...]
    o_ref[0] = xn


def _out_projection(og, w_out, x, gate, final_w, final):
    bsz, t, d = x.shape
    kdim = og.shape[-1]
    tm = min(ROW_TILE, t)
    return pl.pallas_call(
        functools.partial(_outproj_kernel, final=final),
        grid=(bsz, t // tm),
        in_specs=[
            pl.BlockSpec((1, tm, kdim), lambda b, i: (b, i, 0)),
            pl.BlockSpec((kdim, d), lambda b, i: (0, 0)),
            pl.BlockSpec((1, tm, d), lambda b, i: (b, i, 0)),
            pl.BlockSpec((1, 1, d), lambda b, i: (b, 0, 0)),
            pl.BlockSpec((1, d), lambda b, i: (0, 0)),
        ],
        out_specs=pl.BlockSpec((1, tm, d), lambda b, i: (b, i, 0)),
        out_shape=jax.ShapeDtypeStruct((bsz, t, d), F32),
        compiler_params=pltpu.CompilerParams(
            dimension_semantics=("parallel", "parallel"), vmem_limit_bytes=VMEM_LIMIT),
        name="outproj_residual",
    )(og, w_out.astype(BF16), x, gate.reshape(bsz, 1, d), final_w.reshape(1, d))


def _fill_shift_buffer(buf, x_ref, halo_ref, has_prev):
    halo = halo_ref[0, HALO_ROWS - SUBLANE:, :].astype(F32)
    buf[0:SUBLANE, :] = jnp.where(has_prev, halo, 0.0)
    buf[SUBLANE:, :] = x_ref[0].astype(F32)


class _Pipeline:
    def __init__(self, bsz, heads, nt):
        self.bsz, self.heads, self.nt = bsz, heads, nt
        self.blocks = bsz * heads * nt
        self.steps = self.blocks + 1

    def decode(self, blk):
        return blk // (self.heads * self.nt), (blk // self.nt) % self.heads, blk % self.nt

    def local_block(self, l):
        return jnp.minimum(l, self.blocks - 1)

    def chain_block(self, l):
        return jnp.maximum(l - 1, 0)

    def stream(self, shape, col_fn):
        def index(l):
            b, h, i = self.decode(self.local_block(l))
            return b, i, col_fn(h)
        return pl.BlockSpec(shape, index)

    def halo(self, width, col_fn, tb):
        per = tb // HALO_ROWS

        def index(l):
            b, h, i = self.decode(self.local_block(l))
            return b, jnp.maximum(i * per - 1, 0), col_fn(h)
        return pl.BlockSpec((1, HALO_ROWS, width), index)

    def per_head(self, shape, col_fn):
        def index(l):
            _, h, _ = self.decode(self.local_block(l))
            return 0, col_fn(h)
        return pl.BlockSpec(shape, index)

    def chain(self, shape, col_fn):
        def index(l):
            b, h, i = self.decode(self.chain_block(l))
            return b, i, col_fn(h)
        return pl.BlockSpec(shape, index)

    def chain_per_head(self, shape, col_fn):
        def index(l):
            _, h, _ = self.decode(self.chain_block(l))
            return 0, col_fn(h)
        return pl.BlockSpec(shape, index)


def _gdn_kernel(alog_ref, dtb_ref,
                q_ref, k_ref, v_ref, z_ref, ba_ref,
                qh_ref, kh_ref, vh_ref,
                cq_ref, ck_ref, cv_ref, nw_ref,
                o_ref,
                s_scr, bq, bk, bv, u_scr, wq_scr, attn_scr, kdec_scr, egl_scr, *, pipe, dk):
    l = pl.program_id(0)
    _, h, i_local = pipe.decode(pipe.local_block(l))
    heads = pipe.heads
    tb = q_ref.shape[1]
    dv = v_ref.shape[2]
    taps = cq_ref.shape[0]
    slot = lax.rem(l, 2)
    prev = 1 - slot
    rows = tb // GROUPS

    @pl.when(l == 0)
    def _():
        for ref in (s_scr, u_scr, wq_scr, attn_scr, kdec_scr, egl_scr):
            ref[...] = jnp.zeros_like(ref)

    _fill_shift_buffer(bq, q_ref, qh_ref, i_local > 0)
    _fill_shift_buffer(bk, k_ref, kh_ref, i_local > 0)
    _fill_shift_buffer(bv, v_ref, vh_ref, i_local > 0)

    def conv_silu(buf, cw_ref, lo, after):
        cw = _after(cw_ref[...], after)
        base = SUBLANE - (taps - 1) + lo
        acc = buf[base:base + rows, :] * cw[0:1, :]
        for j in range(1, taps):
            acc = acc + buf[base + j:base + j + rows, :] * cw[j:j + 1, :]
        return _silu(acc)

    first_matmul = {}

    def local_work(group):
        lo = group * rows
        chunks = [slice(c * CHUNK, (c + 1) * CHUNK) for c in range(rows // CHUNK)]
        after = first_matmul.get(group - 1)
        q = conv_silu(bq, cq_ref, lo, after)
        k = conv_silu(bk, ck_ref, lo, after)
        v = conv_silu(bv, cv_ref, lo, after)
        q = q * lax.rsqrt(jnp.sum(q * q, axis=-1, keepdims=True) + L2_EPS) * (dk ** -0.5)
        k = k * lax.rsqrt(jnp.sum(k * k, axis=-1, keepdims=True) + L2_EPS)
        ba = ba_ref[0, lo:lo + rows, :].astype(F32)
        lane = lax.broadcasted_iota(jnp.int32, ba.shape, 1)
        bcol = jnp.sum(jnp.where(lane == h, ba, 0.0), axis=1, keepdims=True)
        acol = jnp.sum(jnp.where(lane == h + heads, ba, 0.0), axis=1, keepdims=True)
        beta = _sigmoid(bcol)
        a_scale = jnp.exp(jnp.full((1, 1), alog_ref[h], F32))
        g = -a_scale * _softplus(acol + dtb_ref[h])
        ri, ci = _iotas(CHUNK)
        incl = ri >= ci
        strict = ri > ci
        gc_col, decay = [], []
        for sl in chunks:
            gcol = g[sl]
            g_row = jnp.sum(jnp.where(ri == ci, gcol, 0.0), axis=0, keepdims=True)
            gc_col.append(jnp.sum(jnp.where(incl, g_row, 0.0), axis=1, keepdims=True))
            gc_row = jnp.sum(jnp.where(ri <= ci, gcol, 0.0), axis=0, keepdims=True)
            decay.append(jnp.where(incl, jnp.exp(gc_col[-1] - gc_row), 0.0))
        kb = [k[sl] * beta[sl] for sl in chunks]
        yield
        kq = [_mm_nt(jnp.concatenate([x, q[sl]], axis=0), k[sl]) for x, sl in zip(kb, chunks)]
        first_matmul[group] = kq[0][0:1, 0:1]
        yield
        tinv = yield from _unit_lower_inverses(
            [jnp.where(strict, m[:CHUNK] * d, 0.0) for m, d in zip(kq, decay)], CHUNK)
        e_gc = [jnp.exp(x) for x in gc_col]
        uw = [_mm(ti, jnp.concatenate([v[sl] * beta[sl], x * e], axis=1))
              for ti, sl, x, e in zip(tinv, chunks, kb, e_gc)]
        yield
        for c, sl in enumerate(chunks):
            gc = group * len(chunks) + c
            g_last = gc_col[c][CHUNK - 1:CHUNK, :]
            u_scr[slot, gc] = uw[c][:, :dv]
            wq_scr[slot, gc] = jnp.concatenate([uw[c][:, dv:], q[sl] * e_gc[c]], axis=0).astype(BF16)
            attn_scr[slot, gc] = (kq[c][CHUNK:] * decay[c]).astype(BF16)
            kdec_scr[slot, gc] = (k[sl] * jnp.exp(g_last - gc_col[c])).astype(BF16)
            egl_scr[slot, gc] = jnp.broadcast_to(jnp.exp(g_last), egl_scr.shape[2:])

    def state_chain():
        nw = nw_ref[...]
        s = jnp.where(lax.rem(l - 1, pipe.nt) == 0, 0.0, s_scr[...])
        for c in range(tb // CHUNK):
            sl = slice(c * CHUNK, (c + 1) * CHUNK)
            ws = _mm(wq_scr[prev, c], s)
            yield
            v_new = u_scr[prev, c] - ws[:CHUNK]
            o = ws[CHUNK:] + _mm(attn_scr[prev, c], v_new)
            s = s * egl_scr[prev, c][0:1, 0:1] + _mm_tn(kdec_scr[prev, c], v_new)
            ms = jnp.mean(o * o, axis=-1, keepdims=True)
            on = o * lax.rsqrt(ms + NORM_EPS) * nw
            o_ref[0, sl, :] = (on * _silu(z_ref[0, sl, :].astype(F32))).astype(o_ref.dtype)
            yield
        s_scr[...] = s

    _alternate((state_chain(), 0), *[(local_work(g), 2 * g) for g in range(GROUPS)])


def _gdn_mixer(p, conv_w, a_log, dt_bias, norm_w, heads, dk, dv):
    bsz, t, _ = p.shape
    tb = min(TIME_BLOCK, t)
    nc = tb // CHUNK
    pipe = _Pipeline(bsz, heads, t // tb)
    qk = heads * dk
    kq = qk // dk
    kv = 2 * qk // dv
    kz = (2 * qk + heads * dv) // dv
    kba = (2 * qk + 2 * heads * dv) // LANE
    taps = conv_w.shape[0]
    smem = pl.BlockSpec(memory_space=pltpu.SMEM)
    in_specs = [
        smem, smem,
        pipe.stream((1, tb, dk), lambda h: h),
        pipe.stream((1, tb, dk), lambda h: kq + h),
        pipe.stream((1, tb, dv), lambda h: kv + h),
        pipe.chain((1, tb, dv), lambda h: kz + h),
        pipe.stream((1, tb, LANE), lambda h: kba),
        pipe.halo(dk, lambda h: h, tb),
        pipe.halo(dk, lambda h: kq + h, tb),
        pipe.halo(dv, lambda h: kv + h, tb),
        pipe.per_head((taps, dk), lambda h: h),
        pipe.per_head((taps, dk), lambda h: kq + h),
        pipe.per_head((taps, dv), lambda h: kv + h),
        pl.BlockSpec((1, dv), lambda l: (0, 0)),
    ]
    return pl.pallas_call(
        functools.partial(_gdn_kernel, pipe=pipe, dk=dk),
        grid=(pipe.steps,),
        in_specs=in_specs,
        out_specs=pipe.chain((1, tb, dv), lambda h: h),
        out_shape=jax.ShapeDtypeStruct((bsz, t, heads * dv), BF16),
        scratch_shapes=[
            pltpu.VMEM((dk, dv), F32),
            pltpu.VMEM((tb + SUBLANE, dk), F32),
            pltpu.VMEM((tb + SUBLANE, dk), F32),
            pltpu.VMEM((tb + SUBLANE, dv), F32),
            pltpu.VMEM((2, nc, CHUNK, dv), F32),
            pltpu.VMEM((2, nc, 2 * CHUNK, dk), BF16),
            pltpu.VMEM((2, nc, CHUNK, CHUNK), BF16),
            pltpu.VMEM((2, nc, CHUNK, dk), BF16),
            pltpu.VMEM((2, nc, SUBLANE, LANE), F32),
        ],
        compiler_params=pltpu.CompilerParams(
            dimension_semantics=("arbitrary",), vmem_limit_bytes=VMEM_LIMIT),
        name="gdn_mixer",
    )(a_log.astype(F32), dt_bias.astype(F32), p, p, p, p, p, p, p, p,
      conv_w, conv_w, conv_w, norm_w.reshape(1, dv))


def _hgrn_kernel(q_ref, f_ref, i_ref, z_ref, lb_ref, nw_ref, o_ref, s_scr):
    t = pl.program_id(2)
    tb = q_ref.shape[1]
    dk = q_ref.shape[2]

    @pl.when(t == 0)
    def _():
        s_scr[...] = jnp.zeros_like(s_scr)

    lb = lb_ref[...]
    q = _silu(q_ref[0].astype(F32))
    f = f_ref[0].astype(F32)
    a1 = jnp.log(lb)
    a2 = jnp.log(1.0 - lb) - _softplus(-f)
    log_f = jnp.maximum(a1, a2) + jnp.log(1.0 + jnp.exp(-jnp.abs(a1 - a2)))
    k = (1.0 - lb) * _sigmoid(-f)
    v = i_ref[0].astype(F32)
    z = z_ref[0].astype(F32)
    nw = nw_ref[...]

    ri, ci = _iotas(CHUNK)
    rows = lax.broadcasted_iota(jnp.int32, (CHUNK, 1), 0)
    gcum_all = _chunk_cumsum(log_f)
    chunks = [slice(c * CHUNK, (c + 1) * CHUNK) for c in range(tb // CHUNK)]
    scores, s_mul, q_dec = [], [], []
    for sl in chunks:
        qc, kc = q[sl], k[sl]
        gcum = gcum_all[sl]
        g_last = gcum[CHUNK - 1:CHUNK, :]
        blocks = []
        for i in range(CHUNK // SUB):
            lo, hi = i * SUB, (i + 1) * SUB
            ref = gcum[lo - 1:lo, :] if i > 0 else jnp.zeros_like(g_last)
            qs = qc[lo:hi] * jnp.exp(gcum[lo:hi] - ref)
            ks = jnp.where(rows < hi, kc * jnp.exp(ref - gcum), 0.0)
            blocks.append(_mm_nt(qs, ks))
        scores.append(blocks)
        s_mul.append(_row_to_col(jnp.exp(g_last), dk))
        q_dec.append(qc * jnp.exp(gcum))
    s_add = [_mm_tn(k[sl] * jnp.exp(gcum_all[sl.stop - 1:sl.stop, :] - gcum_all[sl]), v[sl])
             for sl in chunks]
    o_intra = [_mm(jnp.where(ri >= ci, jnp.concatenate(blocks, axis=0), 0.0), v[sl])
               for blocks, sl in zip(scores, chunks)]

    s = s_scr[...]
    for c, sl in enumerate(chunks):
        o = _mm(q_dec[c], s) + o_intra[c]
        s = s * s_mul[c] + s_add[c]
        ms = jnp.mean(o * o, axis=-1, keepdims=True)
        on = o * lax.rsqrt(ms + NORM_EPS) * nw
        o_ref[0, sl, :] = (on * _silu(z[sl])).astype(o_ref.dtype)
    s_scr[...] = s


def _hgrn_mixer(p, lower_bound, norm_w, heads, dk, dv):
    bsz, t, _ = p.shape
    tb = min(TIME_BLOCK, t)
    fdim = heads * dk
    kf = fdim // dk
    ki = 2 * fdim // dv
    kz = (2 * fdim + heads * dv) // dv
    return pl.pallas_call(
        _hgrn_kernel,
        grid=(bsz, heads, t // tb),
        in_specs=[
            pl.BlockSpec((1, tb, dk), lambda b, h, i: (b, i, h)),
            pl.BlockSpec((1, tb, dk), lambda b, h, i: (b, i, kf + h)),
            pl.BlockSpec((1, tb, dv), lambda b, h, i: (b, i, ki + h)),
            pl.BlockSpec((1, tb, dv), lambda b, h, i: (b, i, kz + h)),
            pl.BlockSpec((1, dk), lambda b, h, i: (0, h)),
            pl.BlockSpec((1, dv), lambda b, h, i: (0, 0)),
        ],
        out_specs=pl.BlockSpec((1, tb, dv), lambda b, h, i: (b, i, h)),
        out_shape=jax.ShapeDtypeStruct((bsz, t, heads * dv), BF16),
        scratch_shapes=[pltpu.VMEM((dk, dv), F32)],
        compiler_params=pltpu.CompilerParams(
            dimension_semantics=("parallel", "parallel", "arbitrary"),
            vmem_limit_bytes=VMEM_LIMIT),
        name="hgrn2_mixer",
    )(p, p, p, p, lower_bound.reshape(1, fdim), norm_w.reshape(1, dv))


def _rwkv_kernel(r_ref, k_ref, v_ref, z_ref, wa_ref,
                 rh_ref, kh_ref, vh_ref, zh_ref, wah_ref,
                 mu_r, mu_k, mu_v, mu_z, mu_wa,
                 w0_ref, a0_ref, kk_ref, ka_ref, rk_ref,
                 wl_ref, al_ref, lnw_ref, lnb_ref,
                 o_ref,
                 s_scr, buf_r, buf_k, buf_v, buf_z, buf_wa,
                 wr_scr, u0_scr, arb_scr, y0_scr, bdec_scr, sadd_scr, smul_scr,
                 bonus_scr, gate_scr, *, pipe, dh):
    l = pl.program_id(0)
    _, _, i_local = pipe.decode(pipe.local_block(l))
    tb = r_ref.shape[1]
    lora = wl_ref.shape[0]
    heads = LANE // dh
    n = heads * CHUNK
    slot = lax.rem(l, 2)
    prev = 1 - slot
    rows = tb // RWKV_GROUPS

    @pl.when(l == 0)
    def _():
        for ref in (s_scr, wr_scr, u0_scr, arb_scr, y0_scr, bdec_scr, sadd_scr, smul_scr,
                    bonus_scr, gate_scr):
            ref[...] = jnp.zeros_like(ref)

    for x_ref, halo_ref, b in ((r_ref, rh_ref, buf_r), (k_ref, kh_ref, buf_k),
                               (v_ref, vh_ref, buf_v), (z_ref, zh_ref, buf_z),
                               (wa_ref, wah_ref, buf_wa)):
        _fill_shift_buffer(b, x_ref, halo_ref, i_local > 0)

    def shifted(b, mu_ref, lo, after):
        cur = b[SUBLANE + lo:SUBLANE + lo + rows, :]
        before = b[SUBLANE - 1 + lo:SUBLANE - 1 + lo + rows, :]
        return cur + (before - cur) * _after(mu_ref[...], after)

    first_matmul = {}

    def local_work(group):
        lo = group * rows
        chunks = [slice(c * CHUNK, (c + 1) * CHUNK) for c in range(rows // CHUNK)]
        after = first_matmul.get(group - 1)
        r = shifted(buf_r, mu_r, lo, after)
        k = shifted(buf_k, mu_k, lo, after)
        v = shifted(buf_v, mu_v, lo, after)
        z = shifted(buf_z, mu_z, lo, after)
        wa = shifted(buf_wa, mu_wa, lo, after)
        wd = wa[:, :lora]
        ad = wa[:, lora:]
        w_log = -_softplus(-(w0_ref[...] + _mm(jnp.tanh(wd), wl_ref[...]))) - 0.5
        log_w = -jnp.exp(w_log)
        a = _sigmoid(a0_ref[...] + _mm(ad, al_ref[...]))
        k2 = k * (1.0 + (a - 1.0) * ka_ref[...])
        kkx = k * kk_ref[...]
        kk = kkx * lax.rsqrt(_head_sum(kkx * kkx, dh) + RWKV_L2_EPS)
        bonus_scr[slot, lo:lo + rows, :] = _head_sum(r * k2 * rk_ref[...], dh) * v
        gate_scr[slot, lo:lo + rows, :] = _silu(z)
        kka = kk * a
        gcum = _chunk_cumsum(log_w)
        e_neg = jnp.exp(-gcum)
        a_t = -kk * jnp.exp(gcum - log_w)
        r_t = r * jnp.exp(gcum)
        b_t = kka * e_neg
        k_t = k2 * e_neg
        ri, ci = _iotas(n)
        same_head = _same_block(ri, ci, CHUNK)
        strict = same_head & (ri > ci)
        incl = same_head & (ri >= ci)
        a_bd = [_head_rows(a_t[sl], dh) for sl in chunks]
        r_bd = [_head_rows(r_t[sl], dh) for sl in chunks]
        v_bd = [_head_rows(v[sl], dh) for sl in chunks]
        first = group * len(chunks)
        k_dec = []
        for c, sl in enumerate(chunks):
            g_last = gcum[sl.stop - 1:sl.stop, :]
            e_last = jnp.exp(g_last - gcum[sl])
            wr_scr[slot, first + c, n:, :] = r_bd[c].astype(BF16)
            bdec_scr[slot, first + c] = _head_rows(kka[sl] * e_last, dh).astype(BF16)
            smul_scr[slot, first + c] = jnp.broadcast_to(
                _row_to_col(jnp.exp(g_last), LANE), (LANE, LANE))
            k_dec.append(_head_rows(k2[sl] * e_last, dh))
        yield
        pm = [_mm_nt(jnp.concatenate([a_bd[c], r_bd[c]], axis=0),
                     jnp.concatenate([b_t[sl]] * heads + [k_t[sl]] * heads, axis=0))
              for c, sl in enumerate(chunks)]
        first_matmul[group] = pm[0][0:1, 0:1]
        for c in range(len(chunks)):
            sadd_scr[slot, first + c] = _mm_tn(k_dec[c], v_bd[c])
        yield
        for c, m in enumerate(pm):
            arb_scr[slot, first + c] = jnp.where(incl, m[n:, :n], 0.0).astype(BF16)
        a_krk = [jnp.concatenate([jnp.where(strict, m[:n, n:], 0.0),
                                  jnp.where(incl, m[n:, n:], 0.0)], axis=0) for m in pm]
        av = [_mm(x, vb) for x, vb in zip(a_krk, v_bd)]
        for c, x in enumerate(av):
            y0_scr[slot, first + c] = x[n:]
        av_top = [x[:n].astype(BF16) for x in av]
        tinv = yield from _unit_lower_inverses(
            [jnp.where(strict, -m[:n, :n], 0.0) for m in pm], CHUNK)
        tw = [_mm(ti, jnp.concatenate([ab, x], axis=1)) for ti, ab, x in zip(tinv, a_bd, av_top)]
        yield
        for c in range(len(chunks)):
            wr_scr[slot, first + c, :n, :] = tw[c][:, :LANE].astype(BF16)
            u0_scr[slot, first + c] = tw[c][:, LANE:]
        yield

    def state_chain():
        s = jnp.where(lax.rem(l - 1, pipe.nt) == 0, 0.0, s_scr[...])
        for c in range(tb // CHUNK):
            sl = slice(c * CHUNK, (c + 1) * CHUNK)
            rs = _mm(wr_scr[prev, c], s)
            yield
            u = rs[:n] + u0_scr[prev, c]
            y_bd = rs[n:] + _mm(arb_scr[prev, c], u) + y0_scr[prev, c]
            s = s * smul_scr[prev, c] + _mm_tn(bdec_scr[prev, c], u) + sadd_scr[prev, c]
            y = y_bd[:CHUNK]
            for hh in range(1, heads):
                y = y + y_bd[hh * CHUNK:(hh + 1) * CHUNK]
            mean = _head_sum(y, dh) * (1.0 / dh)
            yc = y - mean
            var = _head_sum(yc * yc, dh) * (1.0 / dh)
            yn = yc * lax.rsqrt(var + dh * RWKV_GN_EPS_PER_DIM)
            out = (yn * lnw_ref[...] + lnb_ref[...] + bonus_scr[prev, sl, :]) * gate_scr[prev, sl, :]
            o_ref[0, sl, :] = out.astype(o_ref.dtype)
            yield
        s_scr[...] = s

    _alternate((state_chain(), 0), *[(local_work(g), 2 * g) for g in range(RWKV_GROUPS)])


def _rwkv_mixer(p, mu, w0, w_lora_up, a0, a_lora_up, k_k, k_a, r_k, ln_w, ln_b, dh):
    bsz, t, _ = p.shape
    branch = w0.shape[0]
    lora = w_lora_up.shape[0]
    assert 2 * lora == LANE and LANE % dh == 0
    tb = min(RWKV_TIME_BLOCK, t)
    nc = tb // CHUNK
    n = (LANE // dh) * CHUNK
    nb = branch // LANE
    pipe = _Pipeline(bsz, nb, t // tb)
    wa_blk = 4 * nb
    in_specs = (
        [pipe.stream((1, tb, LANE), lambda h, j=j: j * nb + h) for j in range(4)]
        + [pipe.stream((1, tb, LANE), lambda h: wa_blk)]
        + [pipe.halo(LANE, lambda h, j=j: j * nb + h, tb) for j in range(4)]
        + [pipe.halo(LANE, lambda h: wa_blk, tb)]
        + [pipe.per_head((1, LANE), lambda h, j=j: j * nb + h) for j in range(4)]
        + [pipe.per_head((1, LANE), lambda h: wa_blk)]
        + [pipe.per_head((1, LANE), lambda h: h)] * 5
        + [pipe.per_head((lora, LANE), lambda h: h)] * 2
        + [pipe.chain_per_head((1, LANE), lambda h: h)] * 2
    )
    row = lambda a: a.reshape(1, -1)
    return pl.pallas_call(
        functools.partial(_rwkv_kernel, pipe=pipe, dh=dh),
        grid=(pipe.steps,),
        in_specs=in_specs,
        out_specs=pipe.chain((1, tb, LANE), lambda h: h),
        out_shape=jax.ShapeDtypeStruct((bsz, t, branch), BF16),
        scratch_shapes=[
            pltpu.VMEM((LANE, LANE), F32),
        ] + [pltpu.VMEM((tb + SUBLANE, LANE), F32)] * 5 + [
            pltpu.VMEM((2, nc, 2 * n, LANE), BF16),
            pltpu.VMEM((2, nc, n, LANE), F32),
            pltpu.VMEM((2, nc, n, n), BF16),
            pltpu.VMEM((2, nc, n, LANE), F32),
            pltpu.VMEM((2, nc, n, LANE), BF16),
            pltpu.VMEM((2, nc, LANE, LANE), F32),
            pltpu.VMEM((2, nc, LANE, LANE), F32),
            pltpu.VMEM((2, tb, LANE), F32),
            pltpu.VMEM((2, tb, LANE), F32),
        ],
        compiler_params=pltpu.CompilerParams(
            dimension_semantics=("arbitrary",), vmem_limit_bytes=VMEM_LIMIT),
        name="rwkv7_mixer",
    )(p, p, p, p, p, p, p, p, p, p,
      row(mu), row(mu), row(mu), row(mu), row(mu),
      row(w0), row(a0), row(k_k), row(k_a), row(r_k),
      w_lora_up, a_lora_up, row(ln_w), row(ln_b))


def kernel(x, c, ada_w, ada_b, norm_w, gdn_w_in, gdn_conv_w, gdn_a_log, gdn_dt_bias, gdn_norm_w, gdn_w_out, hgrn_w_in, hgrn_lb_logits, hgrn_norm_w, hgrn_w_out, rwkv_w_in, rwkv_mu, rwkv_w0, rwkv_w_lora_up, rwkv_a0, rwkv_a_lora_up, rwkv_k_k, rwkv_k_a, rwkv_r_k, rwkv_ln_w, rwkv_ln_b, rwkv_w_out, final_norm_w):
    depth, d = norm_w.shape
    n_mixers = 3
    gdn_heads = gdn_a_log.shape[1]
    gdn_dv = gdn_norm_w.shape[1]
    gdn_dk = (gdn_conv_w.shape[2] - gdn_heads * gdn_dv) // (2 * gdn_heads)
    hgrn_dv = hgrn_norm_w.shape[1]
    hgrn_f = hgrn_lb_logits.shape[1]
    hgrn_heads = (hgrn_w_in.shape[2] - 2 * hgrn_f) // (2 * hgrn_dv)
    hgrn_dk = hgrn_f // hgrn_heads
    rwkv_dh = rwkv_r_k.shape[2]

    x = x.astype(F32)
    mod = _modulations(c, ada_w, ada_b)
    lower_bounds = _lower_bounds(hgrn_lb_logits)
    for i in range(depth):
        shift, scale, gate = mod[i, :, :d], mod[i, :, d:2 * d], mod[i, :, 2 * d:]
        j = i // n_mixers
        kind = i % n_mixers
        if kind == 0:
            p = _in_projection(x, norm_w[i], scale, shift, gdn_w_in[j])
            og = _gdn_mixer(p, gdn_conv_w[j], gdn_a_log[j], gdn_dt_bias[j], gdn_norm_w[j],
                            gdn_heads, gdn_dk, gdn_dv)
            w_out = gdn_w_out[j]
        elif kind == 1:
            p = _in_projection(x, norm_w[i], scale, shift, hgrn_w_in[j])
            og = _hgrn_mixer(p, lower_bounds[i], hgrn_norm_w[j], hgrn_heads, hgrn_dk, hgrn_dv)
            w_out = hgrn_w_out[j]
        else:
            p = _in_projection(x, norm_w[i], scale, shift, rwkv_w_in[j])
            og = _rwkv_mixer(p, rwkv_mu[j], rwkv_w0[j], rwkv_w_lora_up[j], rwkv_a0[j],
                             rwkv_a_lora_up[j], rwkv_k_k[j], rwkv_k_a[j], rwkv_r_k[j],
                             rwkv_ln_w[j], rwkv_ln_b[j], rwkv_dh)
            w_out = rwkv_w_out[j]
        x = _out_projection(og, w_out, x, gate, final_norm_w, final=(i == depth - 1))
    return x
```
